```python
import jax, jax.numpy as jnp
from jax import lax
import numpy as np

D_MODEL = 1024
BATCH = 32
SEQ = 2048
DEPTH = 4
DEC_BATCH = 32
DEC_SEQ = 64
PAST_LEN = 4096

CHUNK = 64
N_META = 16
N_MIXERS = 4
EXPAND = 2
E_WIDTH = EXPAND * D_MODEL
EPS = 1e-6
A_HEADS = 16
A_DK = E_WIDTH // A_HEADS
A_DV = E_WIDTH // A_HEADS
A_BLOCK = 16
B_WIDTH = E_WIDTH
B_BLOCKS = 8
B_BS = B_WIDTH // B_BLOCKS
B_CONV = 4
B_C = 8.0
C_HEADS = 16
C_NOPE = 128
C_ROPE = 64
C_V = E_WIDTH // C_HEADS
C_Q_LORA = 512
C_KV_LORA = 256
C_SCALE = (C_NOPE + C_ROPE) ** -0.5
ROPE_BASE = 10000.0
Q_BLOCK = 128
PAD_CHUNK = 2 ** 30
D_WIDTH = E_WIDTH
D_CONV = 31
NL_A = (DEPTH + 3) // 4
NL_B = (DEPTH + 2) // 4
NL_C = (DEPTH + 1) // 4
NL_D = DEPTH // 4

kernel_name = "hybrid_streaming_encoder_step"

F32 = jnp.float32


def rmsnorm(x, g):
    xf = x.astype(F32)
    y = xf * lax.rsqrt(jnp.mean(xf * xf, axis=-1, keepdims=True) + EPS)
    return (y * g.astype(F32)).astype(x.dtype)


def layernorm(x, g, b):
    xf = x.astype(F32)
    mu = jnp.mean(xf, axis=-1, keepdims=True)
    xc = xf - mu
    var = jnp.mean(xc * xc, axis=-1, keepdims=True)
    return (xc * lax.rsqrt(var + EPS) * g.astype(F32) + b.astype(F32)).astype(x.dtype)


def causal_dwconv(x_ext, w, b):
    c = x_ext.shape[-1]
    out = lax.conv_general_dilated(x_ext, w[:, None, :].astype(x_ext.dtype), window_strides=(1,),
                                   padding="VALID", dimension_numbers=("NWC", "WIO", "NWC"),
                                   feature_group_count=c)
    return out + b.astype(x_ext.dtype)


def rope_tables(pos):
    inv = 1.0 / (ROPE_BASE ** (jnp.arange(0, C_ROPE, 2, dtype=F32) / C_ROPE))
    ang = pos.astype(F32)[:, None] * inv[None, :]
    return jnp.cos(ang), jnp.sin(ang)


def apply_rope(x, cos, sin):
    xf = x.astype(F32)
    x1, x2 = jnp.split(xf, 2, axis=-1)
    return jnp.concatenate([x1 * cos - x2 * sin, x1 * sin + x2 * cos], axis=-1).astype(x.dtype)


def hgrn2_recurrence(q, k, lf, v, s0):
    bsz, t = q.shape[:2]
    nb = -(-t // A_BLOCK)
    pad = nb * A_BLOCK - t

    def blocks(z):
        z = jnp.pad(z, ((0, 0), (0, pad), (0, 0), (0, 0)))
        z = z.reshape(bsz, nb, A_BLOCK, A_HEADS, z.shape[-1])
        return jnp.transpose(z, (1, 0, 3, 2, 4))

    tri = jnp.tril(jnp.ones((A_BLOCK, A_BLOCK), dtype=bool))

    def step(s, inp):
        qb, kb, lfb, vb = inp
        cum = jnp.cumsum(lfb, axis=2)
        qt = qb * jnp.exp(cum)
        kt = kb * jnp.exp(-cum)
        att = jnp.where(tri, jnp.einsum("bhld,bhmd->bhlm", qt, kt), 0.0)
        o = jnp.einsum("bhld,bhdv->bhlv", qt, s) + jnp.einsum("bhlm,bhmv->bhlv", att, vb)
        last = cum[:, :, -1:, :]
        s_new = jnp.exp(last[:, :, 0, :])[..., None] * s + jnp.einsum("bhmd,bhmv->bhdv", kb * jnp.exp(last - cum), vb)
        return s_new, o

    s_fin, o = lax.scan(step, s0, (blocks(q), blocks(k), blocks(lf), blocks(v)))
    o = jnp.transpose(o, (1, 0, 3, 2, 4)).reshape(bsz, nb * A_BLOCK, A_HEADS, A_DV)[:, :t]
    return o, s_fin


def hgrn2_mixer(u, w_in, lb, norm_g, w_out, s0):
    bsz, t, _ = u.shape
    q, f, i, g = jnp.split(u @ w_in, 4, axis=-1)
    lb = lb.astype(F32)
    fg = lb + (1.0 - lb) * jax.nn.sigmoid(f.astype(F32))
    heads = lambda z: z.reshape(bsz, t, A_HEADS, -1)
    o, s_fin = hgrn2_recurrence(heads(jax.nn.silu(q.astype(F32))), heads(1.0 - fg), heads(jnp.log(fg)),
                                heads(i.astype(F32)), s0.astype(F32))
    o = rmsnorm(o, norm_g).reshape(bsz, t, E_WIDTH).astype(u.dtype)
    y = (o * jax.nn.silu(g)) @ w_out
    return y, s_fin.astype(u.dtype)


def rglru_mixer(u, w_in, conv_w, conv_b, wa, ba, wx, bx, lam, w_out, h0, buf, reset_first):
    bsz, t, _ = u.shape
    xb, gb = jnp.split(u @ w_in, 2, axis=-1)
    ext = jnp.concatenate([buf.astype(xb.dtype), xb], axis=1)
    new_buf = ext[:, -(B_CONV - 1):]
    xc = causal_dwconv(ext, conv_w, conv_b)
    xblk = xc.reshape(bsz, t, B_BLOCKS, B_BS)
    r = jax.nn.sigmoid((jnp.einsum("btnc,ncd->btnd", xblk, wa).reshape(bsz, t, B_WIDTH) + ba).astype(F32))
    ig = jax.nn.sigmoid((jnp.einsum("btnc,ncd->btnd", xblk, wx).reshape(bsz, t, B_WIDTH) + bx).astype(F32))
    log_a = -B_C * r * jax.nn.softplus(-lam.astype(F32))
    a = jnp.exp(log_a)
    mult = jnp.sqrt(-jnp.expm1(2.0 * log_a))
    if reset_first:
        mult = mult.at[:, 0].set(1.0)
    bterm = mult * ig * xc.astype(F32)
    bterm = bterm.at[:, 0].add(a[:, 0] * h0.astype(F32))

    def combine(left, right):
        al, bl = left
        ar, br = right
        return al * ar, ar * bl + br

    _, h = lax.associative_scan(combine, (a, bterm), axis=1)
    y = (h.astype(u.dtype) * jax.nn.silu(gb)) @ w_out
    return y, h[:, -1].astype(u.dtype), new_buf


def mla_mixer(u, w_in, q_norm, kv_norm, w_uq, w_uk, w_uv, w_out, cache_c, cache_pe, pos, q_chunk, k_chunk_cache):
    bsz, t, _ = u.shape
    i1 = C_Q_LORA
    i2 = i1 + C_KV_LORA
    i3 = i2 + C_ROPE
    q_lat, kv_lat, k_pe, g = jnp.split(u @ w_in, [i1, i2, i3], axis=-1)
    q = (rmsnorm(q_lat, q_norm) @ w_uq).reshape(bsz, t, C_HEADS, C_NOPE + C_ROPE)
    cos, sin = rope_tables(pos)
    q_nope = q[..., :C_NOPE]
    q_pe = apply_rope(q[..., C_NOPE:], cos[:, None, :], sin[:, None, :])
    c_kv = rmsnorm(kv_lat, kv_norm)
    k_pe = apply_rope(k_pe, cos, sin)
    keys_c = jnp.concatenate([cache_c.astype(c_kv.dtype), c_kv], axis=1)
    keys_pe = jnp.concatenate([cache_pe.astype(k_pe.dtype), k_pe], axis=1)
    k_chunk = jnp.concatenate([k_chunk_cache, q_chunk])
    w_uk_h = w_uk.reshape(C_KV_LORA, C_HEADS, C_NOPE)
    w_uv_h = w_uv.reshape(C_KV_LORA, C_HEADS, C_V)
    nq = -(-t // Q_BLOCK)
    pad = nq * Q_BLOCK - t

    def blocks(z):
        z = jnp.pad(z, ((0, 0), (0, pad), (0, 0), (0, 0)))
        return jnp.moveaxis(z.reshape(bsz, nq, Q_BLOCK, C_HEADS, z.shape[-1]), 1, 0)

    qc = jnp.pad(q_chunk, (0, pad), constant_values=PAD_CHUNK).reshape(nq, Q_BLOCK)

    def attend(args):
        qn, qp, qcb = args
        q_abs = jnp.einsum("bqhn,lhn->bqhl", qn, w_uk_h)
        s = (jnp.einsum("bqhl,bkl->bhqk", q_abs, keys_c, preferred_element_type=F32)
             + jnp.einsum("bqhr,bkr->bhqk", qp, keys_pe, preferred_element_type=F32)) * C_SCALE
        mask = k_chunk[None, :] <= qcb[:, None]
        p = jax.nn.softmax(jnp.where(mask, s, -jnp.inf), axis=-1).astype(keys_c.dtype)
        o_lat = jnp.einsum("bhqk,bkl->bqhl", p, keys_c)
        return jnp.einsum("bqhl,lhv->bqhv", o_lat, w_uv_h)

    o = lax.map(attend, (blocks(q_nope), blocks(q_pe), qc))
    o = jnp.moveaxis(o, 0, 1).reshape(bsz, nq * Q_BLOCK, C_HEADS * C_V)[:, :t]
    y = (o * jax.nn.silu(g)) @ w_out
    return y, c_kv, k_pe


def conformer_mixer(u, w_in, conv_w, conv_b, ln_g, ln_b, w_out, buf):
    a, b, g = jnp.split(u @ w_in, 3, axis=-1)
    v = a * jax.nn.sigmoid(b)
    ext = jnp.concatenate([buf.astype(v.dtype), v], axis=1)
    new_buf = ext[:, -(D_CONV - 1):]
    c = causal_dwconv(ext, conv_w, conv_b)
    c = jax.nn.silu(layernorm(c, ln_g, ln_b))
    y = (c * jax.nn.silu(g)) @ w_out
    return y, new_buf


def trunk(x, pos, q_chunk, k_chunk_cache, reset_first, st, p, lb):
    new = {"hgrn": [], "rg_h": [], "rg_conv": [], "mla_c": [], "mla_pe": [], "conf": []}
    for layer in range(DEPTH):
        kind, j = layer % N_MIXERS, layer // N_MIXERS
        u = rmsnorm(x, p["norm_pre"][layer])
        if kind == 0:
            y, s = hgrn2_mixer(u, p["a_w_in"][j], lb[j], p["a_norm_g"][j], p["a_w_out"][j], st["hgrn"][j])
            new["hgrn"].append(s)
        elif kind == 1:
            y, h, buf = rglru_mixer(u, p["b_w_in"][j], p["b_conv_w"][j], p["b_conv_b"][j], p["b_wa"][j],
                                    p["b_ba"][j], p["b_wx"][j], p["b_bx"][j], p["b_lambda"][j], p["b_w_out"][j],
                                    st["rg_h"][j], st["rg_conv"][j], reset_first)
            new["rg_h"].append(h)
            new["rg_conv"].append(buf)
        elif kind == 2:
            y, c, pe = mla_mixer(u, p["c_w_in"][j], p["c_q_norm"][j], p["c_kv_norm"][j], p["c_w_uq"][j],
                                 p["c_w_uk"][j], p["c_w_uv"][j], p["c_w_out"][j], st["mla_c"][j], st["mla_pe"][j],
                                 pos, q_chunk, k_chunk_cache)
            new["mla_c"].append(c)
            new["mla_pe"].append(pe)
        else:
            y, buf = conformer_mixer(u, p["d_w_in"][j], p["d_conv_w"][j], p["d_conv_b"][j], p["d_ln_g"][j],
                                     p["d_ln_b"][j], p["d_w_out"][j], st["conf"][j])
            new["conf"].append(buf)
        x = x + rmsnorm(y, p["norm_post"][layer])
    return x, new


def setup_inputs(seed: int = 0) -> dict:
    key = jax.random.key(seed)
    ks = iter(list(jax.random.split(key, 48)))

    def nrm(shape, scale):
        return jax.random.normal(next(ks), shape, F32) * scale

    def gain(shape):
        return 1.0 + nrm(shape, 0.05)

    u_lam = jax.random.uniform(next(ks), (NL_B, B_WIDTH), F32, minval=0.9, maxval=0.999)
    s_lam = u_lam ** (1.0 / B_C)
    b_lambda = jnp.log(s_lam) - jnp.log1p(-s_lam)
    return {
        "x_prompt": nrm((BATCH, SEQ, D_MODEL), 1.0),
        "x_sample": nrm((DEC_BATCH, DEC_SEQ, D_MODEL), 1.0),
        "state_hgrn": nrm((NL_A, DEC_BATCH, A_HEADS, A_DK, A_DV), 0.3),
        "state_rglru_h": nrm((NL_B, DEC_BATCH, B_WIDTH), 0.5),
        "state_rglru_conv": nrm((NL_B, DEC_BATCH, B_CONV - 1, B_WIDTH), 1.0),
        "cache_mla_latent": nrm((NL_C, DEC_BATCH, N_META + PAST_LEN, C_KV_LORA), 1.0),
        "cache_mla_rope": nrm((NL_C, DEC_BATCH, N_META + PAST_LEN, C_ROPE), 1.0),
        "state_conformer_conv": nrm((NL_D, DEC_BATCH, D_CONV - 1, D_WIDTH), 0.5),
        "meta_tokens": nrm((N_META, D_MODEL), 1.0),
        "norm_pre": gain((DEPTH, D_MODEL)),
        "norm_post": gain((DEPTH, D_MODEL)),
        "a_w_in": nrm((NL_A, D_MODEL, 4 * E_WIDTH), D_MODEL ** -0.5),
        "a_lb_logits": nrm((NL_A + 1, A_HEADS * A_DK), 0.5),
        "a_norm_g": gain((NL_A, A_DV)),
        "a_w_out": nrm((NL_A, E_WIDTH, D_MODEL), E_WIDTH ** -0.5),
        "b_w_in": nrm((NL_B, D_MODEL, 2 * B_WIDTH), D_MODEL ** -0.5),
        "b_conv_w": nrm((NL_B, B_CONV, B_WIDTH), B_CONV ** -0.5),
        "b_conv_b": nrm((NL_B, B_WIDTH), 0.01),
        "b_wa": nrm((NL_B, B_BLOCKS, B_BS, B_BS), B_BS ** -0.5),
        "b_ba": nrm((NL_B, B_WIDTH), 0.01),
        "b_wx": nrm((NL_B, B_BLOCKS, B_BS, B_BS), B_BS ** -0.5),
        "b_bx": nrm((NL_B, B_WIDTH), 0.01),
        "b_lambda": b_lambda,
        "b_w_out": nrm((NL_B, B_WIDTH, D_MODEL), B_WIDTH ** -0.5),
        "c_w_in": nrm((NL_C, D_MODEL, C_Q_LORA + C_KV_LORA + C_ROPE + C_HEADS * C_V), D_MODEL ** -0.5),
        "c_q_norm": gain((NL_C, C_Q_LORA)),
        "c_kv_norm": gain((NL_C, C_KV_LORA)),
        "c_w_uq": nrm((NL_C, C_Q_LORA, C_HEADS * (C_NOPE + C_ROPE)), C_Q_LORA ** -0.5),
        "c_w_uk": nrm((NL_C, C_KV_LORA, C_HEADS * C_NOPE), C_KV_LORA ** -0.5),
        "c_w_uv": nrm((NL_C, C_KV_LORA, C_HEADS * C_V), C_KV_LORA ** -0.5),
        "c_w_out": nrm((NL_C, C_HEADS * C_V, D_MODEL), (C_HEADS * C_V) ** -0.5),
        "d_w_in": nrm((NL_D, D_MODEL, 3 * D_WIDTH), D_MODEL ** -0.5),
        "d_conv_w": nrm((NL_D, D_CONV, D_WIDTH), D_CONV ** -0.5),
        "d_conv_b": nrm((NL_D, D_WIDTH), 0.01),
        "d_ln_g": gain((NL_D, D_WIDTH)),
        "d_ln_b": nrm((NL_D, D_WIDTH), 0.01),
        "d_w_out": nrm((NL_D, D_WIDTH, D_MODEL), D_WIDTH ** -0.5),
    }


def reference(x_prompt, x_sample, state_hgrn, state_rglru_h, state_rglru_conv, cache_mla_latent, cache_mla_rope,
              state_conformer_conv, meta_tokens, norm_pre, norm_post, a_w_in, a_lb_logits, a_norm_g, a_w_out,
              b_w_in, b_conv_w, b_conv_b, b_wa, b_ba, b_wx, b_bx, b_lambda, b_w_out,
              c_w_in, c_q_norm, c_kv_norm, c_w_uq, c_w_uk, c_w_uv, c_w_out,
              d_w_in, d_conv_w, d_conv_b, d_ln_g, d_ln_b, d_w_out):
    p = {"norm_pre": norm_pre, "norm_post": norm_post,
         "a_w_in": a_w_in, "a_norm_g": a_norm_g, "a_w_out": a_w_out,
         "b_w_in": b_w_in, "b_conv_w": b_conv_w, "b_conv_b": b_conv_b, "b_wa": b_wa, "b_ba": b_ba,
         "b_wx": b_wx, "b_bx": b_bx, "b_lambda": b_lambda, "b_w_out": b_w_out,
         "c_w_in": c_w_in, "c_q_norm": c_q_norm, "c_kv_norm": c_kv_norm, "c_w_uq": c_w_uq,
         "c_w_uk": c_w_uk, "c_w_uv": c_w_uv, "c_w_out": c_w_out,
         "d_w_in": d_w_in, "d_conv_w": d_conv_w, "d_conv_b": d_conv_b, "d_ln_g": d_ln_g, "d_ln_b": d_ln_b,
         "d_w_out": d_w_out}
    lb = jnp.cumsum(jax.nn.softmax(a_lb_logits.astype(F32), axis=0), axis=0)

    bp, dt = x_prompt.shape[0], x_prompt.dtype
    t_p = N_META + x_prompt.shape[1]
    xp = jnp.concatenate([jnp.broadcast_to(meta_tokens.astype(dt)[None], (bp, N_META, D_MODEL)), x_prompt], axis=1)
    pos_p = jnp.arange(t_p, dtype=jnp.int32)
    qc_p = jnp.concatenate([jnp.full((N_META,), -1, jnp.int32),
                            jnp.arange(x_prompt.shape[1], dtype=jnp.int32) // CHUNK])
    st_p = {"hgrn": [jnp.zeros((bp, A_HEADS, A_DK, A_DV), dt) for _ in range(NL_A)],
            "rg_h": [jnp.zeros((bp, B_WIDTH), dt) for _ in range(NL_B)],
            "rg_conv": [jnp.zeros((bp, B_CONV - 1, B_WIDTH), dt) for _ in range(NL_B)],
            "mla_c": [jnp.zeros((bp, 0, C_KV_LORA), dt) for _ in range(NL_C)],
            "mla_pe": [jnp.zeros((bp, 0, C_ROPE), dt) for _ in range(NL_C)],
            "conf": [jnp.zeros((bp, D_CONV - 1, D_WIDTH), dt) for _ in range(NL_D)]}
    yp, newp = trunk(xp, pos_p, qc_p, jnp.zeros((0,), jnp.int32), True, st_p, p, lb)

    past = cache_mla_latent.shape[2] - N_META
    t_s = x_sample.shape[1]
    pos_s = N_META + past + jnp.arange(t_s, dtype=jnp.int32)
    qc_s = jnp.full((t_s,), past // CHUNK, jnp.int32)
    kcc_s = jnp.concatenate([jnp.full((N_META,), -1, jnp.int32), jnp.arange(past, dtype=jnp.int32) // CHUNK])
    st_s = {"hgrn": [state_hgrn[j] for j in range(NL_A)],
            "rg_h": [state_rglru_h[j] for j in range(NL_B)],
            "rg_conv": [state_rglru_conv[j] for j in range(NL_B)],
            "mla_c": [cache_mla_latent[j] for j in range(NL_C)],
            "mla_pe": [cache_mla_rope[j] for j in range(NL_C)],
            "conf": [state_conformer_conv[j] for j in range(NL_D)]}
    ys, news = trunk(x_sample, pos_s, qc_s, kcc_s, False, st_s, p, lb)

    return (yp[:, N_META:], ys,
            jnp.stack(newp["hgrn"]), jnp.stack(news["hgrn"]),
            jnp.stack(newp["rg_h"]), jnp.stack(news["rg_h"]),
            jnp.stack(newp["rg_conv"]), jnp.stack(news["rg_conv"]),
            jnp.stack(newp["mla_c"]), jnp.stack(news["mla_c"]),
            jnp.stack(newp["mla_pe"]), jnp.stack(news["mla_pe"]),
            jnp.stack(newp["conf"]), jnp.stack(news["conf"]))
```

```python
import functools

import jax
import jax.numpy as jnp
from jax import lax
from jax.experimental import pallas as pl
from jax.experimental.pallas import tpu as pltpu

F32 = jnp.float32
BF16 = jnp.bfloat16

EPS = 1e-6
D_MODEL = 1024
E_WIDTH = 2048
N_META = 16
CHUNK = 64
A_HEADS = 16
A_D = 128
A_SUB = 16
A_CHUNK = 128
B_BLOCKS = 8
B_BS = 256
B_CONV = 4
B_C = 8.0
C_HEADS = 16
C_NOPE = 128
C_ROPE = 64
C_V = 128
C_Q_LORA = 512
C_KV_LORA = 256
C_SCALE = (C_NOPE + C_ROPE) ** -0.5
ROPE_BASE = 10000.0
C_PROJ_W = 3072
KEY_BLOCK = 256
D_CONV = 31
D_HIST = 32

VMEM_LIMIT = 48 * 1024 * 1024


def _params(sem):
    return pltpu.CompilerParams(dimension_semantics=sem, vmem_limit_bytes=VMEM_LIMIT)


def _dot(a, b):
    return jnp.dot(a, b, preferred_element_type=F32)


def _dot_nt(a, b):
    return lax.dot_general(a, b, (((1,), (1,)), ((), ())), preferred_element_type=F32)


def _sigmoid(x):
    return jax.nn.sigmoid(x)


def _silu(x):
    return x * jax.nn.sigmoid(x)


def _rms(x, g):
    return x * lax.rsqrt(jnp.mean(x * x, axis=-1, keepdims=True) + EPS) * g


def _norm_proj_kernel(x_ref, g_ref, w_ref, o_ref):
    u = _rms(x_ref[...], g_ref[...]).astype(BF16)
    o_ref[...] = _dot(u, w_ref[...])


def _norm_proj(x2d, g, w):
    m, d = x2d.shape
    n = w.shape[1]
    tm = min(m, 1024)
    tn = n if n <= 2048 else (2048 if n % 2048 == 0 else 1024)
    return pl.pallas_call(
        _norm_proj_kernel,
        grid=(m // tm, n // tn),
        in_specs=[pl.BlockSpec((tm, d), lambda i, j: (i, 0)),
                  pl.BlockSpec((1, d), lambda i, j: (0, 0)),
                  pl.BlockSpec((d, tn), lambda i, j: (0, j))],
        out_specs=pl.BlockSpec((tm, tn), lambda i, j: (i, j)),
        out_shape=jax.ShapeDtypeStruct((m, n), F32),
        compiler_params=_params(("parallel", "arbitrary")),
        name="norm_proj",
    )(x2d, g.reshape(1, d), w)


def _out_res_kernel(h_ref, w_ref, g_ref, x_ref, o_ref):
    y = _dot(h_ref[...], w_ref[...])
    o_ref[...] = x_ref[...] + _rms(y, g_ref[...])


def _out_res(h2d, w, g, x2d):
    m, e = h2d.shape
    d = w.shape[1]
    tm = min(m, 512)
    return pl.pallas_call(
        _out_res_kernel,
        grid=(m // tm,),
        in_specs=[pl.BlockSpec((tm, e), lambda i: (i, 0)),
                  pl.BlockSpec((e, d), lambda i: (0, 0)),
                  pl.BlockSpec((1, d), lambda i: (0, 0)),
                  pl.BlockSpec((tm, d), lambda i: (i, 0))],
        out_specs=pl.BlockSpec((tm, d), lambda i: (i, 0)),
        out_shape=jax.ShapeDtypeStruct((m, d), F32),
        compiler_params=_params(("parallel",)),
        name="out_res",
    )(h2d, w, g.reshape(1, d), x2d)


def _cumsum_rows(x):
    n = x.shape[0]
    row = lax.broadcasted_iota(jnp.int32, x.shape, 0)
    s = 1
    while s < n:
        x = x + jnp.where(row >= s, pltpu.roll(x, s, axis=0), 0.0)
        s *= 2
    return x


def _piecewise_rows(cum, seg, pick):
    c, l = cum.shape
    pieces = []
    for i in range(c // seg):
        r = pick(i)
        if r is None:
            pieces.append(jnp.zeros((seg, l), F32))
        else:
            pieces.append(jnp.broadcast_to(cum[r:r + 1, :], (seg, l)))
    return pieces[0] if len(pieces) == 1 else jnp.concatenate(pieces, axis=0)


def _hgrn_kernel(q_ref, f_ref, i_ref, g_ref, lbl_ref, ng_ref, s0_ref, h_ref, sout_ref, st_ref, *, c, hb, lb_row):
    t = pl.program_id(2)

    @pl.when(t == 0)
    def _():
        for h in range(hb):
            st_ref[h] = s0_ref[0, h].T

    lg = lbl_ref[...]
    e = jnp.exp(lg - jnp.max(lg, axis=0, keepdims=True))
    lb_all = jnp.sum(e[:lb_row + 1], axis=0, keepdims=True) / jnp.sum(e, axis=0, keepdims=True)

    rowi = lax.broadcasted_iota(jnp.int32, (c, c), 0)
    coli = lax.broadcasted_iota(jnp.int32, (c, c), 1)
    diag_mask = (rowi // A_SUB == coli // A_SUB) & (coli <= rowi)
    halves = []
    s = A_SUB
    while 2 * s <= c:
        halves.append(s)
        s *= 2
    row1 = lax.broadcasted_iota(jnp.int32, (c, A_D), 0)

    for h in range(hb):
        sl = slice(h * A_D, (h + 1) * A_D)
        lb = lb_all[:, sl]
        q = _silu(q_ref[0, :, sl])
        fg = lb + (1.0 - lb) * _sigmoid(f_ref[0, :, sl])
        k = 1.0 - fg
        v = i_ref[0, :, sl].astype(BF16)
        cum = _cumsum_rows(jnp.log(fg))

        start = _piecewise_rows(cum, A_SUB, lambda i: None if i == 0 else i * A_SUB - 1)
        rel = cum - start
        att = jnp.where(diag_mask,
                        _dot_nt((q * jnp.exp(rel)).astype(BF16), (k * jnp.exp(-rel)).astype(BF16)), 0.0)
        for s in halves:
            mid = _piecewise_rows(cum, 2 * s, lambda i: i * 2 * s + s - 1)
            second = (row1 % (2 * s)) >= s
            w = jnp.exp(-jnp.abs(cum - mid))
            ql = jnp.where(second, q * w, 0.0).astype(BF16)
            kl = jnp.where(second, 0.0, k * w).astype(BF16)
            lev = _dot_nt(ql, kl)
            if 2 * s < c:
                lev = jnp.where(rowi // (2 * s) == coli // (2 * s), lev, 0.0)
            att = att + lev

        st = st_ref[h]
        o = _dot(att.astype(BF16), v) + _dot_nt((q * jnp.exp(cum)).astype(BF16), st.astype(BF16))
        last = cum[c - 1:c, :]
        kd = (k * jnp.exp(last - cum)).astype(BF16)
        st_ref[h] = st * jnp.exp(last) + _dot(v.T, kd)

        on = _rms(o, ng_ref[...])
        h_ref[0, :, sl] = (on * _silu(g_ref[0, :, sl])).astype(BF16)

    @pl.when(t == pl.num_programs(2) - 1)
    def _():
        for h in range(hb):
            sout_ref[0, h] = st_ref[h].T


def _hgrn_mix(proj, lb_logits, lb_row, norm_g, s0):
    b, t, _ = proj.shape
    c = min(t, A_CHUNK)
    hb = 4
    nhb = A_HEADS // hb
    w = hb * A_D
    bs = s0.shape[0]
    col = lambda part: (lambda bi, hi, ti: (bi, ti, part * nhb + hi))
    kern = functools.partial(_hgrn_kernel, c=c, hb=hb, lb_row=lb_row)
    return pl.pallas_call(
        kern,
        grid=(b, nhb, t // c),
        in_specs=[pl.BlockSpec((1, c, w), col(0)), pl.BlockSpec((1, c, w), col(1)),
                  pl.BlockSpec((1, c, w), col(2)), pl.BlockSpec((1, c, w), col(3)),
                  pl.BlockSpec((lb_logits.shape[0], w), lambda bi, hi, ti: (0, hi)),
                  pl.BlockSpec((1, A_D), lambda bi, hi, ti: (0, 0)),
                  pl.BlockSpec((1, hb, A_D, A_D), lambda bi, hi, ti: (bi if bs > 1 else 0, hi, 0, 0))],
        out_specs=[pl.BlockSpec((1, c, w), lambda bi, hi, ti: (bi, ti, hi)),
                   pl.BlockSpec((1, hb, A_D, A_D), lambda bi, hi, ti: (bi, hi, 0, 0))],
        out_shape=[jax.ShapeDtypeStruct((b, t, E_WIDTH), BF16),
                   jax.ShapeDtypeStruct((b, A_HEADS, A_D, A_D), F32)],
        scratch_shapes=[pltpu.VMEM((hb, A_D, A_D), F32)],
        compiler_params=_params(("parallel", "parallel", "arbitrary")),
        name="hgrn_mix",
    )(proj, proj, proj, proj, lb_logits, norm_g.reshape(1, A_D), s0)


def _expm1(x):
    u = jnp.exp(x)
    um1 = u - 1.0
    safe = (um1 != 0.0) & (u != 0.0)
    lu = jnp.where(safe, jnp.log(jnp.where(safe, u, 2.0)), 1.0)
    return jnp.where(um1 == 0.0, x, jnp.where(u == 0.0, -1.0, um1 * x / lu))


def _rglru_kernel(xb_ref, gb_ref, cw_ref, cb_ref, wa_ref, ba_ref, wx_ref, bx_ref, lam_ref, h0_ref, buf_ref,
                  h_ref, hl_ref, nbuf_ref, ext_s, a_s, b_s, hc_s, *, tt, reset_first):
    t = pl.program_id(1)
    hist = 8

    @pl.when(t == 0)
    def _():
        ext_s[hist - (B_CONV - 1):hist, :] = buf_ref[0]
        hc_s[...] = h0_ref[0]

    ext_s[hist:hist + tt, :] = xb_ref[0]
    xc = cb_ref[...] + cw_ref[0:1, :] * ext_s[hist - 3:hist - 3 + tt, :]
    for k in range(1, B_CONV):
        xc = xc + cw_ref[k:k + 1, :] * ext_s[hist - 3 + k:hist - 3 + k + tt, :]
    nbuf_ref[0] = ext_s[hist + tt - (B_CONV - 1):hist + tt, :]
    ext_s[hist - (B_CONV - 1):hist, :] = ext_s[hist + tt - (B_CONV - 1):hist + tt, :]

    lam = lam_ref[...]
    sp = jnp.maximum(-lam, 0.0) + jnp.log1p(jnp.exp(-jnp.abs(lam)))
    row = lax.broadcasted_iota(jnp.int32, (tt, B_BS), 0)
    for n in range(B_BLOCKS):
        sl = slice(n * B_BS, (n + 1) * B_BS)
        xn = xc[:, sl]
        xnb = xn.astype(BF16)
        r = _sigmoid(_dot(xnb, wa_ref[n]) + ba_ref[:, sl])
        ig = _sigmoid(_dot(xnb, wx_ref[n]) + bx_ref[:, sl])
        log_a = -B_C * r * sp[:, sl]
        mult = jnp.sqrt(-_expm1(2.0 * log_a))
        if reset_first:
            mult = jnp.where((row == 0) & (t == 0), 1.0, mult)
        a_s[:, sl] = jnp.exp(log_a)
        b_s[:, sl] = mult * ig * xn

    row8 = lax.broadcasted_iota(jnp.int32, (8, E_WIDTH), 0)

    def body(g, hc):
        r0 = pl.multiple_of(g * 8, 8)
        a = a_s[pl.ds(r0, 8), :]
        b = b_s[pl.ds(r0, 8), :]
        for s in (1, 2, 4):
            m = row8 >= s
            b = jnp.where(m, a * pltpu.roll(b, s, axis=0) + b, b)
            a = jnp.where(m, a * pltpu.roll(a, s, axis=0), a)
        hrows = a * hc + b
        b_s[pl.ds(r0, 8), :] = hrows
        return hrows[7:8, :]

    hc = lax.fori_loop(0, tt // 8, body, hc_s[...])
    hc_s[...] = hc
    hl_ref[0] = hc
    h_ref[0] = (b_s[...] * _silu(gb_ref[0])).astype(BF16)


def _rglru_mix(proj, conv_w, conv_b, wa, ba, wx, bx, lam, h0, buf, reset_first):
    b, t, _ = proj.shape
    tt = min(t, 256)
    bs = h0.shape[0]
    e = E_WIDTH
    row = lambda a: a.reshape(1, e)
    full = lambda shape: pl.BlockSpec(shape, lambda bi, ti: (0,) * len(shape))
    kern = functools.partial(_rglru_kernel, tt=tt, reset_first=reset_first)
    return pl.pallas_call(
        kern,
        grid=(b, t // tt),
        in_specs=[pl.BlockSpec((1, tt, e), lambda bi, ti: (bi, ti, 0)),
                  pl.BlockSpec((1, tt, e), lambda bi, ti: (bi, ti, 1)),
                  full((B_CONV, e)), full((1, e)),
                  full((B_BLOCKS, B_BS, B_BS)), full((1, e)),
                  full((B_BLOCKS, B_BS, B_BS)), full((1, e)), full((1, e)),
                  pl.BlockSpec((1, 1, e), lambda bi, ti: (bi if bs > 1 else 0, 0, 0)),
                  pl.BlockSpec((1, B_CONV - 1, e), lambda bi, ti: (bi if bs > 1 else 0, 0, 0))],
        out_specs=[pl.BlockSpec((1, tt, e), lambda bi, ti: (bi, ti, 0)),
                   pl.BlockSpec((1, 1, e), lambda bi, ti: (bi, 0, 0)),
                   pl.BlockSpec((1, B_CONV - 1, e), lambda bi, ti: (bi, 0, 0))],
        out_shape=[jax.ShapeDtypeStruct((b, t, e), BF16),
                   jax.ShapeDtypeStruct((b, 1, e), F32),
                   jax.ShapeDtypeStruct((b, B_CONV - 1, e), F32)],
        scratch_shapes=[pltpu.VMEM((8 + tt, e), F32), pltpu.VMEM((tt, e), F32), pltpu.VMEM((tt, e), F32),
                        pltpu.VMEM((1, e), F32)],
        compiler_params=_params(("parallel", "arbitrary")),
        name="rglru_mix",
    )(proj, proj, conv_w, row(conv_b), wa, row(ba), wx, row(bx), row(lam), h0.reshape(bs, 1, e), buf)


def _mla_q_kernel(ql_ref, kv_ref, cos_ref, sin_ref, qn_ref, kvn_ref, wn_ref, wp_ref, wps_ref, wuk_ref,
                  qa_ref, qp_ref, ckv_ref, kpe_ref):
    ql = _rms(ql_ref[0], qn_ref[...]).astype(BF16)
    q_nope = _dot(ql, wn_ref[...])
    q_pe = _dot(ql, wp_ref[...])
    q_sw = _dot(ql, wps_ref[...])
    cos = cos_ref[...]
    sin = sin_ref[...]
    for j in range(C_HEADS // 2):
        sl = slice(j * 128, (j + 1) * 128)
        r = (q_pe[:, sl] * cos + q_sw[:, sl] * sin).astype(BF16)
        qp_ref[0, 2 * j] = r[:, :C_ROPE]
        qp_ref[0, 2 * j + 1] = r[:, C_ROPE:]
    for h in range(C_HEADS):
        qh = q_nope[:, h * C_NOPE:(h + 1) * C_NOPE].astype(BF16)
        qa_ref[0, h] = _dot(qh, wuk_ref[h]).astype(BF16)
    kv = kv_ref[0]
    ckv_ref[0] = _rms(kv[:, :C_KV_LORA], kvn_ref[...])
    k_pe = kv[:, C_KV_LORA:C_KV_LORA + C_ROPE]
    k_sw = kv[:, C_KV_LORA + C_ROPE:C_KV_LORA + 2 * C_ROPE]
    kpe_ref[0] = k_pe * cos[:, :C_ROPE] + k_sw * sin[:, :C_ROPE]


def _mla_q(proj, cos, sin, q_norm, kv_norm, w_nope, w_pe, w_pe_sw, w_uk_h):
    b, t, _ = proj.shape
    tm = min(t, 256)
    full = lambda shape: pl.BlockSpec(shape, lambda bi, ti: (0,) * len(shape))
    return pl.pallas_call(
        _mla_q_kernel,
        grid=(b, t // tm),
        in_specs=[pl.BlockSpec((1, tm, C_Q_LORA), lambda bi, ti: (bi, ti, E_WIDTH // C_Q_LORA)),
                  pl.BlockSpec((1, tm, 512), lambda bi, ti: (bi, ti, (E_WIDTH + C_Q_LORA) // 512)),
                  pl.BlockSpec((tm, 128), lambda bi, ti: (ti, 0)),
                  pl.BlockSpec((tm, 128), lambda bi, ti: (ti, 0)),
                  full((1, C_Q_LORA)), full((1, C_KV_LORA)),
                  full((C_Q_LORA, C_HEADS * C_NOPE)), full((C_Q_LORA, C_HEADS * C_ROPE)),
                  full((C_Q_LORA, C_HEADS * C_ROPE)), full((C_HEADS, C_NOPE, C_KV_LORA))],
        out_specs=[pl.BlockSpec((1, C_HEADS, tm, C_KV_LORA), lambda bi, ti: (bi, 0, ti, 0)),
                   pl.BlockSpec((1, C_HEADS, tm, C_ROPE), lambda bi, ti: (bi, 0, ti, 0)),
                   pl.BlockSpec((1, tm, C_KV_LORA), lambda bi, ti: (bi, ti, 0)),
                   pl.BlockSpec((1, tm, C_ROPE), lambda bi, ti: (bi, ti, 0))],
        out_shape=[jax.ShapeDtypeStruct((b, C_HEADS, t, C_KV_LORA), BF16),
                   jax.ShapeDtypeStruct((b, C_HEADS, t, C_ROPE), BF16),
                   jax.ShapeDtypeStruct((b, t, C_KV_LORA), F32),
                   jax.ShapeDtypeStruct((b, t, C_ROPE), F32)],
        compiler_params=_params(("parallel", "arbitrary")),
        name="mla_q",
    )(proj, proj, cos, sin, q_norm.reshape(1, -1), kv_norm.reshape(1, -1), w_nope, w_pe, w_pe_sw, w_uk_h)


def _mla_attn_kernel(qa_ref, qp_ref, kc_ref, kp_ref, g_ref, wuv_ref, h_ref, m_s, l_s, acc_s, *,
                     tq, n_prefix, n_total, chunked):
    ci = pl.program_id(1)
    nv = (n_prefix + tq * (ci + 1)) if chunked else n_total
    nblk = (nv + KEY_BLOCK - 1) // KEY_BLOCK
    r = C_HEADS * tq
    qa = qa_ref[0].reshape(r, C_KV_LORA)
    qp = qp_ref[0].reshape(r, C_ROPE)
    m_s[...] = jnp.full((r, 1), -jnp.inf, F32)
    l_s[...] = jnp.zeros((r, 1), F32)
    acc_s[...] = jnp.zeros((r, C_KV_LORA), F32)

    def body(j, carry):
        off = pl.multiple_of(j * KEY_BLOCK, KEY_BLOCK)
        kc = kc_ref[0, pl.ds(off, KEY_BLOCK), :]
        kp = kp_ref[0, pl.ds(off, KEY_BLOCK), :]
        s = (_dot_nt(qa, kc) + _dot_nt(qp, kp)) * C_SCALE
        col = off + lax.broadcasted_iota(jnp.int32, (1, KEY_BLOCK), 1)
        s = jnp.where(col < nv, s, -jnp.inf)
        m_old = m_s[...]
        m_new = jnp.maximum(m_old, jnp.max(s, axis=-1, keepdims=True))
        alpha = jnp.exp(m_old - m_new)
        p = jnp.exp(s - m_new)
        l_s[...] = alpha * l_s[...] + jnp.sum(p, axis=-1, keepdims=True)
        acc_s[...] = alpha * acc_s[...] + _dot(p.astype(BF16), kc)
        m_s[...] = m_new
        return carry

    lax.fori_loop(0, nblk, body, 0)
    o_lat = (acc_s[...] / l_s[...]).astype(BF16)
    for h in range(C_HEADS):
        oh = _dot(o_lat[h * tq:(h + 1) * tq], wuv_ref[h])
        sl = slice(h * C_V, (h + 1) * C_V)
        h_ref[0, :, sl] = (oh * _silu(g_ref[0, :, sl])).astype(BF16)


def _mla_attn(qa, qp, kc, kp, proj, w_uv_h, n_prefix, chunked):
    b, _, t, _ = qa.shape
    tq = min(t, CHUNK)
    n_total = n_prefix + t
    tk = kc.shape[1]
    r = C_HEADS * tq
    kern = functools.partial(_mla_attn_kernel, tq=tq, n_prefix=n_prefix, n_total=n_total, chunked=chunked)
    return pl.pallas_call(
        kern,
        grid=(b, t // tq),
        in_specs=[pl.BlockSpec((1, C_HEADS, tq, C_KV_LORA), lambda bi, ci: (bi, 0, ci, 0)),
                  pl.BlockSpec((1, C_HEADS, tq, C_ROPE), lambda bi, ci: (bi, 0, ci, 0)),
                  pl.BlockSpec((1, tk, C_KV_LORA), lambda bi, ci: (bi, 0, 0)),
                  pl.BlockSpec((1, tk, C_ROPE), lambda bi, ci: (bi, 0, 0)),
                  pl.BlockSpec((1, tq, E_WIDTH), lambda bi, ci: (bi, ci, 0)),
                  pl.BlockSpec((C_HEADS, C_KV_LORA, C_V), lambda bi, ci: (0, 0, 0))],
        out_specs=pl.BlockSpec((1, tq, E_WIDTH), lambda bi, ci: (bi, ci, 0)),
        out_shape=jax.ShapeDtypeStruct((b, t, E_WIDTH), BF16),
        scratch_shapes=[pltpu.VMEM((r, 1), F32), pltpu.VMEM((r, 1), F32), pltpu.VMEM((r, C_KV_LORA), F32)],
        compiler_params=_params(("parallel", "arbitrary")),
        name="mla_attn",
    )(qa, qp, kc, kp, proj, w_uv_h)


def _conf_kernel(a_ref, b_ref, g_ref, cw_ref, cb_ref, lg_ref, lb_ref, buf_ref, h_ref, nbuf_ref, ext_s, conv_s, *,
                 tt):
    t = pl.program_id(1)
    nhist = D_CONV - 1
    lead = D_HIST - nhist

    @pl.when(t == 0)
    def _():
        ext_s[lead:D_HIST, :] = buf_ref[0]

    ext_s[D_HIST:D_HIST + tt, :] = a_ref[0] * _sigmoid(b_ref[0])

    cw = 256
    rc = min(tt, 64)

    def col_body(ci, carry):
        c0 = pl.multiple_of(ci * cw, cw)
        for r0 in range(0, tt, rc):
            acc = jnp.broadcast_to(cb_ref[:, pl.ds(c0, cw)], (rc, cw))
            for k in range(D_CONV):
                acc = acc + cw_ref[k:k + 1, pl.ds(c0, cw)] * ext_s[lead + k + r0:lead + k + r0 + rc, pl.ds(c0, cw)]
            conv_s[r0:r0 + rc, pl.ds(c0, cw)] = acc
        return carry

    lax.fori_loop(0, E_WIDTH // cw, col_body, 0)

    nbuf_ref[0] = ext_s[tt + lead:tt + D_HIST, :]
    ext_s[0:D_HIST, :] = ext_s[tt:tt + D_HIST, :]

    c = conv_s[...]
    mu = jnp.mean(c, axis=-1, keepdims=True)
    xc = c - mu
    var = jnp.mean(xc * xc, axis=-1, keepdims=True)
    y = _silu(xc * lax.rsqrt(var + EPS) * lg_ref[...] + lb_ref[...])
    h_ref[0] = (y * _silu(g_ref[0])).astype(BF16)


def _conf_mix(proj, conv_w, conv_b, ln_g, ln_b, buf):
    b, t, _ = proj.shape
    tt = min(t, 256)
    bs = buf.shape[0]
    e = E_WIDTH
    row = lambda a: a.reshape(1, e)
    full = lambda shape: pl.BlockSpec(shape, lambda bi, ti: (0,) * len(shape))
    kern = functools.partial(_conf_kernel, tt=tt)
    return pl.pallas_call(
        kern,
        grid=(b, t // tt),
        in_specs=[pl.BlockSpec((1, tt, e), lambda bi, ti: (bi, ti, 0)),
                  pl.BlockSpec((1, tt, e), lambda bi, ti: (bi, ti, 1)),
                  pl.BlockSpec((1, tt, e), lambda bi, ti: (bi, ti, 2)),
                  full((D_CONV, e)), full((1, e)), full((1, e)), full((1, e)),
                  pl.BlockSpec((1, D_CONV - 1, e), lambda bi, ti: (bi if bs > 1 else 0, 0, 0))],
        out_specs=[pl.BlockSpec((1, tt, e), lambda bi, ti: (bi, ti, 0)),
                   pl.BlockSpec((1, D_CONV - 1, e), lambda bi, ti: (bi, 0, 0))],
        out_shape=[jax.ShapeDtypeStruct((b, t, e), BF16),
                   jax.ShapeDtypeStruct((b, D_CONV - 1, e), F32)],
        scratch_shapes=[pltpu.VMEM((D_HIST + tt, e), F32), pltpu.VMEM((tt, e), F32)],
        compiler_params=_params(("parallel", "arbitrary")),
        name="conf_mix",
    )(proj, proj, proj, conv_w, row(conv_b), row(ln_g), row(ln_b), buf)


def _rope_tables(pos0, t):
    inv = 1.0 / (ROPE_BASE ** (jnp.arange(0, C_ROPE, 2, dtype=F32) / C_ROPE))
    ang = (pos0 + jnp.arange(t, dtype=jnp.int32)).astype(F32)[:, None] * inv[None, :]
    cos, sin = jnp.cos(ang), jnp.sin(ang)
    return jnp.tile(cos, (1, 4)), jnp.tile(jnp.concatenate([-sin, sin], axis=-1), (1, 2))


def _pad_keys(k):
    tk = k.shape[1]
    tkp = -(-tk // KEY_BLOCK) * KEY_BLOCK
    return jnp.pad(k.astype(BF16), ((0, 0), (0, tkp - tk), (0, 0)))


def _trunk(x, w, st, *, pos0, reset_first, chunked):
    b, t, d = x.shape
    m = b * t
    new = {}

    def layer(x, idx, mix):
        proj = _norm_proj(x.reshape(m, d), w["norm_pre"][idx], w["w_in"][idx])
        h = mix(proj.reshape(b, t, -1))
        return _out_res(h.reshape(m, E_WIDTH), w["w_out"][idx], w["norm_post"][idx], x.reshape(m, d)).reshape(b, t, d)

    def mix_a(proj):
        h, new["hgrn"] = _hgrn_mix(proj, w["a_lb_logits"], 0, w["a_norm_g"], st["hgrn"])
        return h

    def mix_b(proj):
        h, hl, new["rg_conv"] = _rglru_mix(proj, w["b_conv_w"], w["b_conv_b"], w["b_wa"], w["b_ba"], w["b_wx"],
                                           w["b_bx"], w["b_lambda"], st["rg_h"], st["rg_conv"], reset_first)
        new["rg_h"] = hl.reshape(b, E_WIDTH)
        return h

    def mix_c(proj):
        cos, sin = _rope_tables(pos0, t)
        qa, qp, ckv, kpe = _mla_q(proj, cos, sin, w["c_q_norm"], w["c_kv_norm"], w["c_w_nope"], w["c_w_pe"],
                                  w["c_w_pe_sw"], w["c_w_uk_h"])
        new["mla_c"], new["mla_pe"] = ckv, kpe
        kc, kp = ckv, kpe
        n_prefix = 0
        if st["mla_c"] is not None:
            pc, pp = st["mla_c"], st["mla_pe"]
            n_prefix = pc.shape[1]
            pc = jnp.broadcast_to(pc, (b,) + pc.shape[1:])
            pp = jnp.broadcast_to(pp, (b,) + pp.shape[1:])
            kc = jnp.concatenate([pc.astype(BF16), ckv.astype(BF16)], axis=1)
            kp = jnp.concatenate([pp.astype(BF16), kpe.astype(BF16)], axis=1)
        return _mla_attn(qa, qp, _pad_keys(kc), _pad_keys(kp), proj, w["c_w_uv_h"], n_prefix, chunked)

    def mix_d(proj):
        h, new["conf"] = _conf_mix(proj, w["d_conv_w"], w["d_conv_b"], w["d_ln_g"], w["d_ln_b"], st["conf"])
        return h

    for idx, mix in enumerate((mix_a, mix_b, mix_c, mix_d)):
        x = layer(x, idx, mix)
    return x, new


def kernel(x_prompt, x_sample, state_hgrn, state_rglru_h, state_rglru_conv, cache_mla_latent, cache_mla_rope, state_conformer_conv, meta_tokens, norm_pre, norm_post, a_w_in, a_lb_logits, a_norm_g, a_w_out, b_w_in, b_conv_w, b_conv_b, b_wa, b_ba, b_wx, b_bx, b_lambda, b_w_out, c_w_in, c_q_norm, c_kv_norm, c_w_uq, c_w_uk, c_w_uv, c_w_out, d_w_in, d_conv_w, d_conv_b, d_ln_g, d_ln_b, d_w_out):
    assert norm_pre.shape[0] == 4, "one layer of each mixer type"
    bf = lambda a: a.astype(BF16)

    c_in = c_w_in[0]
    i1, i2, i3 = C_Q_LORA, C_Q_LORA + C_KV_LORA, C_Q_LORA + C_KV_LORA + C_ROPE
    half = C_ROPE // 2
    k_pe_cols = c_in[:, i2:i3]
    k_pe_sw = jnp.concatenate([k_pe_cols[:, half:], k_pe_cols[:, :half]], axis=1)
    pad = jnp.zeros((D_MODEL, C_PROJ_W - (E_WIDTH + i3 + C_ROPE)), c_in.dtype)
    c_in_perm = jnp.concatenate([c_in[:, i3:], c_in[:, :i3], k_pe_sw, pad], axis=1)
    uq = c_w_uq[0].reshape(C_Q_LORA, C_HEADS, C_NOPE + C_ROPE)
    uq_pe = uq[:, :, C_NOPE:]
    uq_pe_sw = jnp.concatenate([uq_pe[..., half:], uq_pe[..., :half]], axis=-1)

    w = {
        "norm_pre": norm_pre, "norm_post": norm_post,
        "w_in": [bf(a_w_in[0]), bf(b_w_in[0]), bf(c_in_perm), bf(d_w_in[0])],
        "w_out": [bf(a_w_out[0]), bf(b_w_out[0]), bf(c_w_out[0]), bf(d_w_out[0])],
        "a_lb_logits": a_lb_logits, "a_norm_g": a_norm_g[0],
        "b_conv_w": b_conv_w[0], "b_conv_b": b_conv_b[0], "b_wa": bf(b_wa[0]), "b_ba": b_ba[0],
        "b_wx": bf(b_wx[0]), "b_bx": b_bx[0], "b_lambda": b_lambda[0],
        "c_q_norm": c_q_norm[0], "c_kv_norm": c_kv_norm[0],
        "c_w_nope": bf(uq[:, :, :C_NOPE].reshape(C_Q_LORA, C_HEADS * C_NOPE)),
        "c_w_pe": bf(uq_pe.reshape(C_Q_LORA, C_HEADS * C_ROPE)),
        "c_w_pe_sw": bf(uq_pe_sw.reshape(C_Q_LORA, C_HEADS * C_ROPE)),
        "c_w_uk_h": bf(jnp.transpose(c_w_uk[0].reshape(C_KV_LORA, C_HEADS, C_NOPE), (1, 2, 0))),
        "c_w_uv_h": bf(jnp.transpose(c_w_uv[0].reshape(C_KV_LORA, C_HEADS, C_V), (1, 0, 2))),
        "d_conv_w": d_conv_w[0], "d_conv_b": d_conv_b[0], "d_ln_g": d_ln_g[0], "d_ln_b": d_ln_b[0],
    }

    bp = x_prompt.shape[0]
    dt = x_prompt.dtype

    st_m = {"hgrn": jnp.zeros((1, A_HEADS, A_D, A_D), dt), "rg_h": jnp.zeros((1, E_WIDTH), dt),
            "rg_conv": jnp.zeros((1, B_CONV - 1, E_WIDTH), dt), "mla_c": None, "mla_pe": None,
            "conf": jnp.zeros((1, D_CONV - 1, E_WIDTH), dt)}
    _, new_m = _trunk(meta_tokens.astype(dt)[None], w, st_m, pos0=0, reset_first=True, chunked=False)

    st_p = {"hgrn": new_m["hgrn"], "rg_h": new_m["rg_h"], "rg_conv": new_m["rg_conv"],
            "mla_c": new_m["mla_c"], "mla_pe": new_m["mla_pe"], "conf": new_m["conf"]}
    yp, new_p = _trunk(x_prompt, w, st_p, pos0=N_META, reset_first=False, chunked=True)

    st_s = {"hgrn": state_hgrn[0], "rg_h": state_rglru_h[0], "rg_conv": state_rglru_conv[0],
            "mla_c": cache_mla_latent[0], "mla_pe": cache_mla_rope[0], "conf": state_conformer_conv[0]}
    ys, new_s = _trunk(x_sample, w, st_s, pos0=cache_mla_latent.shape[2], reset_first=False, chunked=False)

    def with_meta(meta_rows, rows):
        return jnp.concatenate([jnp.broadcast_to(meta_rows, (bp,) + meta_rows.shape[1:]), rows], axis=1)

    return (yp, ys,
            new_p["hgrn"][None], new_s["hgrn"][None],
            new_p["rg_h"][None], new_s["rg_h"][None],
            new_p["rg_conv"][None], new_s["rg_conv"][None],
            with_meta(new_m["mla_c"], new_p["mla_c"])[None], new_s["mla_c"][None],
            with_meta(new_m["mla_pe"], new_p["mla_pe"])[None], new_s["mla_pe"][None],
            new_p["conf"][None], new_s["conf"][None])
```

```python
import functools

import jax
import jax.numpy as jnp
from jax import lax
from jax.experimental import pallas as pl
from jax.experimental.pallas import tpu as pltpu

F32 = jnp.float32
BF16 = jnp.bfloat16

EPS = 1e-6
D_MODEL = 1024
E_WIDTH = 2048
N_META = 16
CHUNK = 64
A_HEADS = 16
A_D = 128
A_SUB = 16
A_CHUNK = 128
B_BLOCKS = 8
B_BS = 256
B_CONV = 4
B_C = 8.0
C_HEADS = 16
C_NOPE = 128
C_ROPE = 64
C_V = 128
C_Q_LORA = 512
C_KV_LORA = 256
C_SCALE = (C_NOPE + C_ROPE) ** -0.5
LOG2E = 1.4426950408889634
ROPE_BASE = 10000.0
C_PROJ_W = 3072
KEY_BLOCK = 256
D_CONV = 31
D_HIST = 32

VMEM_LIMIT = 48 * 1024 * 1024


def _params(sem):
    return pltpu.CompilerParams(dimension_semantics=sem, vmem_limit_bytes=VMEM_LIMIT)


def _dot(a, b):
    return jnp.dot(a, b, preferred_element_type=F32)


def _dot_nt(a, b):
    return lax.dot_general(a, b, (((1,), (1,)), ((), ())), preferred_element_type=F32)


def _sigmoid(x):
    return jax.nn.sigmoid(x)


def _silu(x):
    return x * jax.nn.sigmoid(x)


def _rms(x, g):
    return x * lax.rsqrt(jnp.mean(x * x, axis=-1, keepdims=True) + EPS) * g


def _norm_proj_kernel(x_ref, g_ref, w_ref, o_ref):
    u = _rms(x_ref[...], g_ref[...]).astype(BF16)
    o_ref[...] = _dot(u, w_ref[...])


def _norm_proj(x2d, g, w):
    m, d = x2d.shape
    n = w.shape[1]
    tm = min(m, 1024)
    tn = n if n <= 2048 else (2048 if n % 2048 == 0 else 1024)
    return pl.pallas_call(
        _norm_proj_kernel,
        grid=(m // tm, n // tn),
        in_specs=[pl.BlockSpec((tm, d), lambda i, j: (i, 0)),
                  pl.BlockSpec((1, d), lambda i, j: (0, 0)),
                  pl.BlockSpec((d, tn), lambda i, j: (0, j))],
        out_specs=pl.BlockSpec((tm, tn), lambda i, j: (i, j)),
        out_shape=jax.ShapeDtypeStruct((m, n), F32),
        compiler_params=_params(("parallel", "arbitrary")),
        name="norm_proj",
    )(x2d, g.reshape(1, d), w)


def _out_res_kernel(h_ref, w_ref, g_ref, x_ref, o_ref):
    y = _dot(h_ref[...], w_ref[...])
    o_ref[...] = x_ref[...] + _rms(y, g_ref[...])


def _out_res(h2d, w, g, x2d):
    m, e = h2d.shape
    d = w.shape[1]
    tm = min(m, 512)
    return pl.pallas_call(
        _out_res_kernel,
        grid=(m // tm,),
        in_specs=[pl.BlockSpec((tm, e), lambda i: (i, 0)),
                  pl.BlockSpec((e, d), lambda i: (0, 0)),
                  pl.BlockSpec((1, d), lambda i: (0, 0)),
                  pl.BlockSpec((tm, d), lambda i: (i, 0))],
        out_specs=pl.BlockSpec((tm, d), lambda i: (i, 0)),
        out_shape=jax.ShapeDtypeStruct((m, d), F32),
        compiler_params=_params(("parallel",)),
        name="out_res",
    )(h2d, w, g.reshape(1, d), x2d)


def _cumsum_rows(x):
    n = x.shape[0]
    row = lax.broadcasted_iota(jnp.int32, x.shape, 0)
    s = 1
    while s < n:
        x = x + jnp.where(row >= s, pltpu.roll(x, s, axis=0), 0.0)
        s *= 2
    return x


def _piecewise_rows(cum, seg, pick):
    c, l = cum.shape
    pieces = []
    for i in range(c // seg):
        r = pick(i)
        if r is None:
            pieces.append(jnp.zeros((seg, l), F32))
        else:
            pieces.append(jnp.broadcast_to(cum[r:r + 1, :], (seg, l)))
    return pieces[0] if len(pieces) == 1 else jnp.concatenate(pieces, axis=0)


def _hgrn_kernel(q_ref, f_ref, i_ref, g_ref, lbl_ref, ng_ref, s0_ref, h_ref, sout_ref, st_ref, *, c, nch, hb,
                 lb_row):
    t = pl.program_id(2)

    @pl.when(t == 0)
    def _():
        for h in range(hb):
            st_ref[h] = s0_ref[0, h].T

    lg = lbl_ref[...]
    e = jnp.exp(lg - jnp.max(lg, axis=0, keepdims=True))
    lb_all = jnp.sum(e[:lb_row + 1], axis=0, keepdims=True) / jnp.sum(e, axis=0, keepdims=True)

    rowi = lax.broadcasted_iota(jnp.int32, (c, c), 0)
    coli = lax.broadcasted_iota(jnp.int32, (c, c), 1)
    diag_mask = (rowi // A_SUB == coli // A_SUB) & (coli <= rowi)
    halves = []
    s = A_SUB
    while 2 * s <= c:
        halves.append(s)
        s *= 2
    row1 = lax.broadcasted_iota(jnp.int32, (c, A_D), 0)

    for ch, h in [(ch, h) for ch in range(nch) for h in range(hb)]:
        sl = slice(h * A_D, (h + 1) * A_D)
        rows = slice(ch * c, (ch + 1) * c)
        lb = lb_all[:, sl]
        q = _silu(q_ref[0, rows, sl])
        fg = lb + (1.0 - lb) * _sigmoid(f_ref[0, rows, sl])
        k = 1.0 - fg
        v = i_ref[0, rows, sl].astype(BF16)
        cum = _cumsum_rows(jnp.log(fg))

        start = _piecewise_rows(cum, A_SUB, lambda i: None if i == 0 else i * A_SUB - 1)
        rel = cum - start
        att = jnp.where(diag_mask,
                        _dot_nt((q * jnp.exp(rel)).astype(BF16), (k * jnp.exp(-rel)).astype(BF16)), 0.0)
        for s in halves:
            mid = _piecewise_rows(cum, 2 * s, lambda i: i * 2 * s + s - 1)
            second = (row1 % (2 * s)) >= s
            w = jnp.exp(-jnp.abs(cum - mid))
            ql = jnp.where(second, q * w, 0.0).astype(BF16)
            kl = jnp.where(second, 0.0, k * w).astype(BF16)
            lev = _dot_nt(ql, kl)
            if 2 * s < c:
                lev = jnp.where(rowi // (2 * s) == coli // (2 * s), lev, 0.0)
            att = att + lev

        st = st_ref[h]
        o = _dot(att.astype(BF16), v) + _dot_nt((q * jnp.exp(cum)).astype(BF16), st.astype(BF16))
        last = cum[c - 1:c, :]
        kd = (k * jnp.exp(last - cum)).astype(BF16)
        st_ref[h] = st * jnp.exp(last) + _dot(v.T, kd)

        on = _rms(o, ng_ref[...])
        h_ref[0, rows, sl] = (on * _silu(g_ref[0, rows, sl])).astype(BF16)

    @pl.when(t == pl.num_programs(2) - 1)
    def _():
        for h in range(hb):
            sout_ref[0, h] = st_ref[h].T


def _hgrn_mix(proj, lb_logits, lb_row, norm_g, s0):
    b, t, _ = proj.shape
    c = min(t, A_CHUNK)
    nch = 2 if t % (2 * c) == 0 else 1
    tt = nch * c
    hb = 4
    nhb = A_HEADS // hb
    w = hb * A_D
    bs = s0.shape[0]
    col = lambda part: (lambda bi, hi, ti: (bi, ti, part * nhb + hi))
    kern = functools.partial(_hgrn_kernel, c=c, nch=nch, hb=hb, lb_row=lb_row)
    return pl.pallas_call(
        kern,
        grid=(b, nhb, t // tt),
        in_specs=[pl.BlockSpec((1, tt, w), col(0)), pl.BlockSpec((1, tt, w), col(1)),
                  pl.BlockSpec((1, tt, w), col(2)), pl.BlockSpec((1, tt, w), col(3)),
                  pl.BlockSpec((lb_logits.shape[0], w), lambda bi, hi, ti: (0, hi)),
                  pl.BlockSpec((1, A_D), lambda bi, hi, ti: (0, 0)),
                  pl.BlockSpec((1, hb, A_D, A_D), lambda bi, hi, ti: (bi if bs > 1 else 0, hi, 0, 0))],
        out_specs=[pl.BlockSpec((1, tt, w), lambda bi, hi, ti: (bi, ti, hi)),
                   pl.BlockSpec((1, hb, A_D, A_D), lambda bi, hi, ti: (bi, hi, 0, 0))],
        out_shape=[jax.ShapeDtypeStruct((b, t, E_WIDTH), BF16),
                   jax.ShapeDtypeStruct((b, A_HEADS, A_D, A_D), F32)],
        scratch_shapes=[pltpu.VMEM((hb, A_D, A_D), F32)],
        compiler_params=_params(("parallel", "parallel", "arbitrary")),
        name="hgrn_mix",
    )(proj, proj, proj, proj, lb_logits, norm_g.reshape(1, A_D), s0)


def _rglru_kernel(xb_ref, gb_ref, cw_ref, cb_ref, wa_ref, ba_ref, wx_ref, bx_ref, lam_ref, h0_ref, buf_ref,
                  h_ref, hl_ref, nbuf_ref, ext_s, a_s, b_s, hc_s, *, tt, reset_first):
    t = pl.program_id(1)
    hist = 8

    @pl.when(t == 0)
    def _():
        ext_s[hist - (B_CONV - 1):hist, :] = buf_ref[0]
        hc_s[...] = h0_ref[0]

    ext_s[hist:hist + tt, :] = xb_ref[0]
    xc = cb_ref[...] + cw_ref[0:1, :] * ext_s[hist - 3:hist - 3 + tt, :]
    for k in range(1, B_CONV):
        xc = xc + cw_ref[k:k + 1, :] * ext_s[hist - 3 + k:hist - 3 + k + tt, :]
    nbuf_ref[0] = ext_s[hist + tt - (B_CONV - 1):hist + tt, :]
    ext_s[hist - (B_CONV - 1):hist, :] = ext_s[hist + tt - (B_CONV - 1):hist + tt, :]

    lam = lam_ref[...]
    sp = jnp.maximum(-lam, 0.0) + jnp.log1p(jnp.exp(-jnp.abs(lam)))
    row = lax.broadcasted_iota(jnp.int32, (tt, B_BS), 0)
    for n in range(B_BLOCKS):
        sl = slice(n * B_BS, (n + 1) * B_BS)
        xn = xc[:, sl]
        xnb = xn.astype(BF16)
        r = _sigmoid(_dot(xnb, wa_ref[n]) + ba_ref[:, sl])
        ig = _sigmoid(_dot(xnb, wx_ref[n]) + bx_ref[:, sl])
        log_a = -B_C * r * sp[:, sl]
        a = jnp.exp(log_a)
        mult = jnp.sqrt(jnp.tanh(-log_a) * (1.0 + a * a))
        if reset_first:
            mult = jnp.where((row == 0) & (t == 0), 1.0, mult)
        a_s[:, sl] = a
        b_s[:, sl] = mult * ig * xn

    row8 = lax.broadcasted_iota(jnp.int32, (8, E_WIDTH), 0)

    def body(g, hc):
        r0 = pl.multiple_of(g * 8, 8)
        a = a_s[pl.ds(r0, 8), :]
        b = b_s[pl.ds(r0, 8), :]
        for s in (1, 2, 4):
            m = row8 >= s
            b = jnp.where(m, a * pltpu.roll(b, s, axis=0) + b, b)
            a = jnp.where(m, a * pltpu.roll(a, s, axis=0), a)
        hrows = a * hc + b
        b_s[pl.ds(r0, 8), :] = hrows
        return hrows[7:8, :]

    hc = lax.fori_loop(0, tt // 8, body, hc_s[...])
    hc_s[...] = hc
    hl_ref[0] = hc
    h_ref[0] = (b_s[...] * _silu(gb_ref[0])).astype(BF16)


def _rglru_mix(proj, conv_w, conv_b, wa, ba, wx, bx, lam, h0, buf, reset_first):
    b, t, _ = proj.shape
    tt = min(t, 256)
    bs = h0.shape[0]
    e = E_WIDTH
    row = lambda a: a.reshape(1, e)
    full = lambda shape: pl.BlockSpec(shape, lambda bi, ti: (0,) * len(shape))
    kern = functools.partial(_rglru_kernel, tt=tt, reset_first=reset_first)
    return pl.pallas_call(
        kern,
        grid=(b, t // tt),
        in_specs=[pl.BlockSpec((1, tt, e), lambda bi, ti: (bi, ti, 0)),
                  pl.BlockSpec((1, tt, e), lambda bi, ti: (bi, ti, 1)),
                  full((B_CONV, e)), full((1, e)),
                  full((B_BLOCKS, B_BS, B_BS)), full((1, e)),
                  full((B_BLOCKS, B_BS, B_BS)), full((1, e)), full((1, e)),
                  pl.BlockSpec((1, 1, e), lambda bi, ti: (bi if bs > 1 else 0, 0, 0)),
                  pl.BlockSpec((1, B_CONV - 1, e), lambda bi, ti: (bi if bs > 1 else 0, 0, 0))],
        out_specs=[pl.BlockSpec((1, tt, e), lambda bi, ti: (bi, ti, 0)),
                   pl.BlockSpec((1, 1, e), lambda bi, ti: (bi, 0, 0)),
                   pl.BlockSpec((1, B_CONV - 1, e), lambda bi, ti: (bi, 0, 0))],
        out_shape=[jax.ShapeDtypeStruct((b, t, e), BF16),
                   jax.ShapeDtypeStruct((b, 1, e), F32),
                   jax.ShapeDtypeStruct((b, B_CONV - 1, e), F32)],
        scratch_shapes=[pltpu.VMEM((8 + tt, e), F32), pltpu.VMEM((tt, e), F32), pltpu.VMEM((tt, e), F32),
                        pltpu.VMEM((1, e), F32)],
        compiler_params=_params(("parallel", "arbitrary")),
        name="rglru_mix",
    )(proj, proj, conv_w, row(conv_b), wa, row(ba), wx, row(bx), row(lam), h0.reshape(bs, 1, e), buf)


def _mla_q_kernel(ql_ref, kv_ref, cos_ref, sin_ref, qn_ref, kvn_ref, wn_ref, wp_ref, wps_ref, wuk_ref,
                  qa_ref, qp_ref, ckv_ref, kpe_ref):
    ql = _rms(ql_ref[0], qn_ref[...]).astype(BF16)
    q_nope = _dot(ql, wn_ref[...])
    q_pe = _dot(ql, wp_ref[...])
    q_sw = _dot(ql, wps_ref[...])
    cos = cos_ref[...]
    sin = sin_ref[...]
    for j in range(C_HEADS // 2):
        sl = slice(j * 128, (j + 1) * 128)
        r = (q_pe[:, sl] * cos + q_sw[:, sl] * sin).astype(BF16)
        qp_ref[0, 2 * j] = r[:, :C_ROPE]
        qp_ref[0, 2 * j + 1] = r[:, C_ROPE:]
    for h in range(C_HEADS):
        qh = q_nope[:, h * C_NOPE:(h + 1) * C_NOPE].astype(BF16)
        qa_ref[0, h] = _dot(qh, wuk_ref[h]).astype(BF16)
    kv = kv_ref[0]
    ckv_ref[0] = _rms(kv[:, :C_KV_LORA], kvn_ref[...])
    k_pe = kv[:, C_KV_LORA:C_KV_LORA + C_ROPE]
    k_sw = kv[:, C_KV_LORA + C_ROPE:C_KV_LORA + 2 * C_ROPE]
    kpe_ref[0] = k_pe * cos[:, :C_ROPE] + k_sw * sin[:, :C_ROPE]


def _mla_q(proj, cos, sin, q_norm, kv_norm, w_nope, w_pe, w_pe_sw, w_uk_h):
    b, t, _ = proj.shape
    tm = min(t, 256)
    full = lambda shape: pl.BlockSpec(shape, lambda bi, ti: (0,) * len(shape))
    return pl.pallas_call(
        _mla_q_kernel,
        grid=(b, t // tm),
        in_specs=[pl.BlockSpec((1, tm, C_Q_LORA), lambda bi, ti: (bi, ti, E_WIDTH // C_Q_LORA)),
                  pl.BlockSpec((1, tm, 512), lambda bi, ti: (bi, ti, (E_WIDTH + C_Q_LORA) // 512)),
                  pl.BlockSpec((tm, 128), lambda bi, ti: (ti, 0)),
                  pl.BlockSpec((tm, 128), lambda bi, ti: (ti, 0)),
                  full((1, C_Q_LORA)), full((1, C_KV_LORA)),
                  full((C_Q_LORA, C_HEADS * C_NOPE)), full((C_Q_LORA, C_HEADS * C_ROPE)),
                  full((C_Q_LORA, C_HEADS * C_ROPE)), full((C_HEADS, C_NOPE, C_KV_LORA))],
        out_specs=[pl.BlockSpec((1, C_HEADS, tm, C_KV_LORA), lambda bi, ti: (bi, 0, ti, 0)),
                   pl.BlockSpec((1, C_HEADS, tm, C_ROPE), lambda bi, ti: (bi, 0, ti, 0)),
                   pl.BlockSpec((1, tm, C_KV_LORA), lambda bi, ti: (bi, ti, 0)),
                   pl.BlockSpec((1, tm, C_ROPE), lambda bi, ti: (bi, ti, 0))],
        out_shape=[jax.ShapeDtypeStruct((b, C_HEADS, t, C_KV_LORA), BF16),
                   jax.ShapeDtypeStruct((b, C_HEADS, t, C_ROPE), BF16),
                   jax.ShapeDtypeStruct((b, t, C_KV_LORA), F32),
                   jax.ShapeDtypeStruct((b, t, C_ROPE), F32)],
        compiler_params=_params(("parallel", "arbitrary")),
        name="mla_q",
    )(proj, proj, cos, sin, q_norm.reshape(1, -1), kv_norm.reshape(1, -1), w_nope, w_pe, w_pe_sw, w_uk_h)


def _mla_attn_kernel(qa_ref, qp_ref, kc_ref, kct_ref, kp_ref, g_ref, wuv_ref, h_ref, acc_s, *,
                     tq, n_prefix, n_total, chunked):
    ci = pl.program_id(1)
    nv = (n_prefix + tq * (ci + 1)) if chunked else n_total
    nblk = (nv + KEY_BLOCK - 1) // KEY_BLOCK
    r = C_HEADS * tq
    qa = qa_ref[0].reshape(r, C_KV_LORA)
    qp = qp_ref[0].reshape(r, C_ROPE)
    acc_s[...] = jnp.zeros((C_KV_LORA, r), F32)
    c2 = C_SCALE * LOG2E

    def body(j, carry):
        m_old, l_old = carry
        off = pl.multiple_of(j * KEY_BLOCK, KEY_BLOCK)
        kc = kc_ref[0, pl.ds(off, KEY_BLOCK), :]
        kp = kp_ref[0, pl.ds(off, KEY_BLOCK), :]
        s = _dot_nt(kc, qa) + _dot_nt(kp, qp)
        key = off + lax.broadcasted_iota(jnp.int32, (KEY_BLOCK, 1), 0)
        s = jnp.where(key < nv, s, -jnp.inf)
        m_new = jnp.maximum(m_old, jnp.max(s, axis=0, keepdims=True))
        alpha = jnp.exp2((m_old - m_new) * c2)
        p = jnp.exp2((s - m_new) * c2)
        l_new = alpha * l_old + jnp.sum(p, axis=0, keepdims=True)
        acc_s[...] = alpha * acc_s[...] + _dot(kct_ref[0, :, pl.ds(off, KEY_BLOCK)], p.astype(BF16))
        return m_new, l_new

    init = (jnp.full((1, r), -jnp.inf, F32), jnp.zeros((1, r), F32))
    _, l_fin = lax.fori_loop(0, nblk, body, init)
    o_lat = (acc_s[...] / l_fin).T.astype(BF16)
    for h in range(C_HEADS):
        oh = _dot(o_lat[h * tq:(h + 1) * tq], wuv_ref[h])
        sl = slice(h * C_V, (h + 1) * C_V)
        h_ref[0, :, sl] = (oh * _silu(g_ref[0, :, sl])).astype(BF16)


def _mla_attn(qa, qp, kc, kp, proj, w_uv_h, n_prefix, chunked):
    b, _, t, _ = qa.shape
    tq = min(t, CHUNK)
    n_total = n_prefix + t
    tk = kc.shape[1]
    r = C_HEADS * tq
    kct = jnp.swapaxes(kc, 1, 2)
    kern = functools.partial(_mla_attn_kernel, tq=tq, n_prefix=n_prefix, n_total=n_total, chunked=chunked)
    return pl.pallas_call(
        kern,
        grid=(b, t // tq),
        in_specs=[pl.BlockSpec((1, C_HEADS, tq, C_KV_LORA), lambda bi, ci: (bi, 0, ci, 0)),
                  pl.BlockSpec((1, C_HEADS, tq, C_ROPE), lambda bi, ci: (bi, 0, ci, 0)),
                  pl.BlockSpec((1, tk, C_KV_LORA), lambda bi, ci: (bi, 0, 0)),
                  pl.BlockSpec((1, C_KV_LORA, tk), lambda bi, ci: (bi, 0, 0)),
                  pl.BlockSpec((1, tk, C_ROPE), lambda bi, ci: (bi, 0, 0)),
                  pl.BlockSpec((1, tq, E_WIDTH), lambda bi, ci: (bi, ci, 0)),
                  pl.BlockSpec((C_HEADS, C_KV_LORA, C_V), lambda bi, ci: (0, 0, 0))],
        out_specs=pl.BlockSpec((1, tq, E_WIDTH), lambda bi, ci: (bi, ci, 0)),
        out_shape=jax.ShapeDtypeStruct((b, t, E_WIDTH), BF16),
        scratch_shapes=[pltpu.VMEM((C_KV_LORA, r), F32)],
        compiler_params=_params(("parallel", "arbitrary")),
        name="mla_attn",
    )(qa, qp, kc, kct, kp, proj, w_uv_h)


def _conf_kernel(a_ref, b_ref, g_ref, cw_ref, cb_ref, lg_ref, lb_ref, buf_ref, h_ref, nbuf_ref, ext_s, conv_s, sh_s,
                 *, tt):
    t = pl.program_id(1)
    nhist = D_CONV - 1
    lead = D_HIST - nhist

    @pl.when(t == 0)
    def _():
        ext_s[lead:D_HIST, :] = buf_ref[0]

    ext_s[D_HIST:D_HIST + tt, :] = a_ref[0] * _sigmoid(b_ref[0])

    cw = 256
    rc = min(tt, 64)
    nsh = tt + D_HIST - 8

    def col_body(ci, carry):
        c0 = pl.multiple_of(ci * cw, cw)
        for r in range(1, 8):
            sh_s[r - 1] = ext_s[r:r + nsh, pl.ds(c0, cw)]
        for r0 in range(0, tt, rc):
            acc = jnp.broadcast_to(cb_ref[:, pl.ds(c0, cw)], (rc, cw))
            for k in range(D_CONV):
                j8, r = divmod(lead + k, 8)
                lo = 8 * j8 + r0
                win = ext_s[lo:lo + rc, pl.ds(c0, cw)] if r == 0 else sh_s[r - 1, lo:lo + rc, :]
                acc = acc + cw_ref[k:k + 1, pl.ds(c0, cw)] * win
            conv_s[r0:r0 + rc, pl.ds(c0, cw)] = acc
        return carry

    lax.fori_loop(0, E_WIDTH // cw, col_body, 0)

    nbuf_ref[0] = ext_s[tt + lead:tt + D_HIST, :]
    ext_s[0:D_HIST, :] = ext_s[tt:tt + D_HIST, :]

    rn = 16

    def norm_body(i, carry):
        r0 = pl.multiple_of(i * rn, rn)
        c = conv_s[pl.ds(r0, rn), :]
        mu = jnp.mean(c, axis=-1, keepdims=True)
        xc = c - mu
        var = jnp.mean(xc * xc, axis=-1, keepdims=True)
        y = _silu(xc * lax.rsqrt(var + EPS) * lg_ref[...] + lb_ref[...])
        h_ref[0, pl.ds(r0, rn), :] = (y * _silu(g_ref[0, pl.ds(r0, rn), :])).astype(BF16)
        return carry

    lax.fori_loop(0, tt // rn, norm_body, 0, unroll=min(4, tt // rn))


def _conf_mix(proj, conv_w, conv_b, ln_g, ln_b, buf):
    b, t, _ = proj.shape
    tt = min(t, 256)
    bs = buf.shape[0]
    e = E_WIDTH
    row = lambda a: a.reshape(1, e)
    full = lambda shape: pl.BlockSpec(shape, lambda bi, ti: (0,) * len(shape))
    kern = functools.partial(_conf_kernel, tt=tt)
    return pl.pallas_call(
        kern,
        grid=(b, t // tt),
        in_specs=[pl.BlockSpec((1, tt, e), lambda bi, ti: (bi, ti, 0)),
                  pl.BlockSpec((1, tt, e), lambda bi, ti: (bi, ti, 1)),
                  pl.BlockSpec((1, tt, e), lambda bi, ti: (bi, ti, 2)),
                  full((D_CONV, e)), full((1, e)), full((1, e)), full((1, e)),
                  pl.BlockSpec((1, D_CONV - 1, e), lambda bi, ti: (bi if bs > 1 else 0, 0, 0))],
        out_specs=[pl.BlockSpec((1, tt, e), lambda bi, ti: (bi, ti, 0)),
                   pl.BlockSpec((1, D_CONV - 1, e), lambda bi, ti: (bi, 0, 0))],
        out_shape=[jax.ShapeDtypeStruct((b, t, e), BF16),
                   jax.ShapeDtypeStruct((b, D_CONV - 1, e), F32)],
        scratch_shapes=[pltpu.VMEM((D_HIST + tt, e), F32), pltpu.VMEM((tt, e), F32),
                        pltpu.VMEM((7, tt + D_HIST - 8, 256), F32)],
        compiler_params=_params(("parallel", "arbitrary")),
        name="conf_mix",
    )(proj, proj, proj, conv_w, row(conv_b), row(ln_g), row(ln_b), buf)


def _rope_tables(pos0, t):
    inv = 1.0 / (ROPE_BASE ** (jnp.arange(0, C_ROPE, 2, dtype=F32) / C_ROPE))
    ang = (pos0 + jnp.arange(t, dtype=jnp.int32)).astype(F32)[:, None] * inv[None, :]
    cos, sin = jnp.cos(ang), jnp.sin(ang)
    return jnp.tile(cos, (1, 4)), jnp.tile(jnp.concatenate([-sin, sin], axis=-1), (1, 2))


def _pad_keys(k):
    tk = k.shape[1]
    tkp = -(-tk // KEY_BLOCK) * KEY_BLOCK
    return jnp.pad(k.astype(BF16), ((0, 0), (0, tkp - tk), (0, 0)))


def _trunk(x, w, st, *, pos0, reset_first, chunked):
    b, t, d = x.shape
    m = b * t
    new = {}

    def layer(x, idx, mix):
        proj = _norm_proj(x.reshape(m, d), w["norm_pre"][idx], w["w_in"][idx])
        h = mix(proj.reshape(b, t, -1))
        return _out_res(h.reshape(m, E_WIDTH), w["w_out"][idx], w["norm_post"][idx], x.reshape(m, d)).reshape(b, t, d)

    def mix_a(proj):
        h, new["hgrn"] = _hgrn_mix(proj, w["a_lb_logits"], 0, w["a_norm_g"], st["hgrn"])
        return h

    def mix_b(proj):
        h, hl, new["rg_conv"] = _rglru_mix(proj, w["b_conv_w"], w["b_conv_b"], w["b_wa"], w["b_ba"], w["b_wx"],
                                           w["b_bx"], w["b_lambda"], st["rg_h"], st["rg_conv"], reset_first)
        new["rg_h"] = hl.reshape(b, E_WIDTH)
        return h

    def mix_c(proj):
        cos, sin = _rope_tables(pos0, t)
        qa, qp, ckv, kpe = _mla_q(proj, cos, sin, w["c_q_norm"], w["c_kv_norm"], w["c_w_nope"], w["c_w_pe"],
                                  w["c_w_pe_sw"], w["c_w_uk_h"])
        new["mla_c"], new["mla_pe"] = ckv, kpe
        kc, kp = ckv, kpe
        n_prefix = 0
        if st["mla_c"] is not None:
            pc, pp = st["mla_c"], st["mla_pe"]
            n_prefix = pc.shape[1]
            pc = jnp.broadcast_to(pc, (b,) + pc.shape[1:])
            pp = jnp.broadcast_to(pp, (b,) + pp.shape[1:])
            kc = jnp.concatenate([pc.astype(BF16), ckv.astype(BF16)], axis=1)
            kp = jnp.concatenate([pp.astype(BF16), kpe.astype(BF16)], axis=1)
        return _mla_attn(qa, qp, _pad_keys(kc), _pad_keys(kp), proj, w["c_w_uv_h"], n_prefix, chunked)

    def mix_d(proj):
        h, new["conf"] = _conf_mix(proj, w["d_conv_w"], w["d_conv_b"], w["d_ln_g"], w["d_ln_b"], st["conf"])
        return h

    for idx, mix in enumerate((mix_a, mix_b, mix_c, mix_d)):
        x = layer(x, idx, mix)
    return x, new


def kernel(x_prompt, x_sample, state_hgrn, state_rglru_h, state_rglru_conv, cache_mla_latent, cache_mla_rope, state_conformer_conv, meta_tokens, norm_pre, norm_post, a_w_in, a_lb_logits, a_norm_g, a_w_out, b_w_in, b_conv_w, b_conv_b, b_wa, b_ba, b_wx, b_bx, b_lambda, b_w_out, c_w_in, c_q_norm, c_kv_norm, c_w_uq, c_w_uk, c_w_uv, c_w_out, d_w_in, d_conv_w, d_conv_b, d_ln_g, d_ln_b, d_w_out):
    assert norm_pre.shape[0] == 4, "one layer of each mixer type"
    bf = lambda a: a.astype(BF16)

    c_in = c_w_in[0]
    i1, i2, i3 = C_Q_LORA, C_Q_LORA + C_KV_LORA, C_Q_LORA + C_KV_LORA + C_ROPE
    half = C_ROPE // 2
    k_pe_cols = c_in[:, i2:i3]
    k_pe_sw = jnp.concatenate([k_pe_cols[:, half:], k_pe_cols[:, :half]], axis=1)
    pad = jnp.zeros((D_MODEL, C_PROJ_W - (E_WIDTH + i3 + C_ROPE)), c_in.dtype)
    c_in_perm = jnp.concatenate([c_in[:, i3:], c_in[:, :i3], k_pe_sw, pad], axis=1)
    uq = c_w_uq[0].reshape(C_Q_LORA, C_HEADS, C_NOPE + C_ROPE)
    uq_pe = uq[:, :, C_NOPE:]
    uq_pe_sw = jnp.concatenate([uq_pe[..., half:], uq_pe[..., :half]], axis=-1)

    w = {
        "norm_pre": norm_pre, "norm_post": norm_post,
        "w_in": [bf(a_w_in[0]), bf(b_w_in[0]), bf(c_in_perm), bf(d_w_in[0])],
        "w_out": [bf(a_w_out[0]), bf(b_w_out[0]), bf(c_w_out[0]), bf(d_w_out[0])],
        "a_lb_logits": a_lb_logits, "a_norm_g": a_norm_g[0],
        "b_conv_w": b_conv_w[0], "b_conv_b": b_conv_b[0], "b_wa": bf(b_wa[0]), "b_ba": b_ba[0],
        "b_wx": bf(b_wx[0]), "b_bx": b_bx[0], "b_lambda": b_lambda[0],
        "c_q_norm": c_q_norm[0], "c_kv_norm": c_kv_norm[0],
        "c_w_nope": bf(uq[:, :, :C_NOPE].reshape(C_Q_LORA, C_HEADS * C_NOPE)),
        "c_w_pe": bf(uq_pe.reshape(C_Q_LORA, C_HEADS * C_ROPE)),
        "c_w_pe_sw": bf(uq_pe_sw.reshape(C_Q_LORA, C_HEADS * C_ROPE)),
        "c_w_uk_h": bf(jnp.transpose(c_w_uk[0].reshape(C_KV_LORA, C_HEADS, C_NOPE), (1, 2, 0))),
        "c_w_uv_h": bf(jnp.transpose(c_w_uv[0].reshape(C_KV_LORA, C_HEADS, C_V), (1, 0, 2))),
        "d_conv_w": d_conv_w[0], "d_conv_b": d_conv_b[0], "d_ln_g": d_ln_g[0], "d_ln_b": d_ln_b[0],
    }

    bp = x_prompt.shape[0]
    dt = x_prompt.dtype

    st_m = {"hgrn": jnp.zeros((1, A_HEADS, A_D, A_D), dt), "rg_h": jnp.zeros((1, E_WIDTH), dt),
            "rg_conv": jnp.zeros((1, B_CONV - 1, E_WIDTH), dt), "mla_c": None, "mla_pe": None,
            "conf": jnp.zeros((1, D_CONV - 1, E_WIDTH), dt)}
    _, new_m = _trunk(meta_tokens.astype(dt)[None], w, st_m, pos0=0, reset_first=True, chunked=False)

    st_p = {"hgrn": new_m["hgrn"], "rg_h": new_m["rg_h"], "rg_conv": new_m["rg_conv"],
            "mla_c": new_m["mla_c"], "mla_pe": new_m["mla_pe"], "conf": new_m["conf"]}
    yp, new_p = _trunk(x_prompt, w, st_p, pos0=N_META, reset_first=False, chunked=True)

    st_s = {"hgrn": state_hgrn[0], "rg_h": state_rglru_h[0], "rg_conv": state_rglru_conv[0],
            "mla_c": cache_mla_latent[0], "mla_pe": cache_mla_rope[0], "conf": state_conformer_conv[0]}
    ys, new_s = _trunk(x_sample, w, st_s, pos0=cache_mla_latent.shape[2], reset_first=False, chunked=False)

    def with_meta(meta_rows, rows):
        return jnp.concatenate([jnp.broadcast_to(meta_rows, (bp,) + meta_rows.shape[1:]), rows], axis=1)

    return (yp, ys,
            new_p["hgrn"][None], new_s["hgrn"][None],
            new_p["rg_h"][None], new_s["rg_h"][None],
            new_p["rg_conv"][None], new_s["rg_conv"][None],
            with_meta(new_m["mla_c"], new_p["mla_c"])[None], new_s["mla_c"][None],
            with_meta(new_m["mla_pe"], new_p["mla_pe"])[None], new_s["mla_pe"][None],
            new_p["conf"][None], new_s["conf"][None])
```

```python
import functools

import jax
import jax.numpy as jnp
from jax import lax
from jax.experimental import pallas as pl
from jax.experimental.pallas import tpu as pltpu

F32 = jnp.float32
BF16 = jnp.bfloat16

EPS = 1e-6
D_MODEL = 1024
E_WIDTH = 2048
N_META = 16
CHUNK = 64
A_HEADS = 16
A_D = 128
A_SUB = 16
A_CHUNK = 128
B_BLOCKS = 8
B_BS = 256
B_CONV = 4
B_C = 8.0
C_HEADS = 16
C_NOPE = 128
C_ROPE = 64
C_V = 128
C_Q_LORA = 512
C_KV_LORA = 256
C_SCALE = (C_NOPE + C_ROPE) ** -0.5
LOG2E = 1.4426950408889634
ROPE_BASE = 10000.0
C_PROJ_W = 3072
KEY_BLOCK = 256
D_CONV = 31
D_HIST = 32

VMEM_LIMIT = 48 * 1024 * 1024


def _params(sem):
    return pltpu.CompilerParams(dimension_semantics=sem, vmem_limit_bytes=VMEM_LIMIT)


def _dot(a, b):
    return jnp.dot(a, b, preferred_element_type=F32)


def _dot_nt(a, b):
    return lax.dot_general(a, b, (((1,), (1,)), ((), ())), preferred_element_type=F32)


def _sigmoid(x):
    return jax.nn.sigmoid(x)


def _silu(x):
    return x * jax.nn.sigmoid(x)


def _rms(x, g):
    return x * lax.rsqrt(jnp.mean(x * x, axis=-1, keepdims=True) + EPS) * g


def _norm_proj_kernel(x_ref, g_ref, w_ref, o_ref):
    u = _rms(x_ref[...], g_ref[...]).astype(BF16)
    o_ref[...] = _dot(u, w_ref[...])


def _norm_proj(x2d, g, w):
    m, d = x2d.shape
    n = w.shape[1]
    tm = min(m, 1024)
    tn = n if n <= 2048 else (2048 if n % 2048 == 0 else 1024)
    return pl.pallas_call(
        _norm_proj_kernel,
        grid=(m // tm, n // tn),
        in_specs=[pl.BlockSpec((tm, d), lambda i, j: (i, 0)),
                  pl.BlockSpec((1, d), lambda i, j: (0, 0)),
                  pl.BlockSpec((d, tn), lambda i, j: (0, j))],
        out_specs=pl.BlockSpec((tm, tn), lambda i, j: (i, j)),
        out_shape=jax.ShapeDtypeStruct((m, n), F32),
        compiler_params=_params(("parallel", "arbitrary")),
        name="norm_proj",
    )(x2d, g.reshape(1, d), w)


def _out_res_kernel(h_ref, w_ref, g_ref, x_ref, o_ref):
    y = _dot(h_ref[...], w_ref[...])
    o_ref[...] = x_ref[...] + _rms(y, g_ref[...])


def _out_res(h2d, w, g, x2d):
    m, e = h2d.shape
    d = w.shape[1]
    tm = min(m, 512)
    return pl.pallas_call(
        _out_res_kernel,
        grid=(m // tm,),
        in_specs=[pl.BlockSpec((tm, e), lambda i: (i, 0)),
                  pl.BlockSpec((e, d), lambda i: (0, 0)),
                  pl.BlockSpec((1, d), lambda i: (0, 0)),
                  pl.BlockSpec((tm, d), lambda i: (i, 0))],
        out_specs=pl.BlockSpec((tm, d), lambda i: (i, 0)),
        out_shape=jax.ShapeDtypeStruct((m, d), F32),
        compiler_params=_params(("parallel",)),
        name="out_res",
    )(h2d, w, g.reshape(1, d), x2d)


def _cumsum_rows(x):
    n = x.shape[0]
    row = lax.broadcasted_iota(jnp.int32, x.shape, 0)
    s = 1
    while s < n:
        x = x + jnp.where(row >= s, pltpu.roll(x, s, axis=0), 0.0)
        s *= 2
    return x


def _piecewise_rows(cum, seg, pick):
    c, l = cum.shape
    pieces = []
    for i in range(c // seg):
        r = pick(i)
        if r is None:
            pieces.append(jnp.zeros((seg, l), F32))
        else:
            pieces.append(jnp.broadcast_to(cum[r:r + 1, :], (seg, l)))
    return pieces[0] if len(pieces) == 1 else jnp.concatenate(pieces, axis=0)


def _hgrn_kernel(q_ref, f_ref, i_ref, g_ref, lbl_ref, ng_ref, s0_ref, h_ref, sout_ref, st_ref, *, c, nch, hb,
                 lb_row):
    t = pl.program_id(2)

    @pl.when(t == 0)
    def _():
        for h in range(hb):
            st_ref[h] = s0_ref[0, h].T

    lg = lbl_ref[...]
    e = jnp.exp(lg - jnp.max(lg, axis=0, keepdims=True))
    lb_all = jnp.sum(e[:lb_row + 1], axis=0, keepdims=True) / jnp.sum(e, axis=0, keepdims=True)

    rowi = lax.broadcasted_iota(jnp.int32, (c, c), 0)
    coli = lax.broadcasted_iota(jnp.int32, (c, c), 1)
    diag_mask = (rowi // A_SUB == coli // A_SUB) & (coli <= rowi)
    halves = []
    s = A_SUB
    while 2 * s <= c:
        halves.append(s)
        s *= 2
    row1 = lax.broadcasted_iota(jnp.int32, (c, A_D), 0)

    for ch, h in [(ch, h) for ch in range(nch) for h in range(hb)]:
        sl = slice(h * A_D, (h + 1) * A_D)
        rows = slice(ch * c, (ch + 1) * c)
        lb = lb_all[:, sl]
        q = _silu(q_ref[0, rows, sl])
        fg = lb + (1.0 - lb) * _sigmoid(f_ref[0, rows, sl])
        k = 1.0 - fg
        v = i_ref[0, rows, sl].astype(BF16)
        cum = _cumsum_rows(jnp.log(fg))

        start = _piecewise_rows(cum, A_SUB, lambda i: None if i == 0 else i * A_SUB - 1)
        rel = cum - start
        att = jnp.where(diag_mask,
                        _dot_nt((q * jnp.exp(rel)).astype(BF16), (k * jnp.exp(-rel)).astype(BF16)), 0.0)
        for s in halves:
            mid = _piecewise_rows(cum, 2 * s, lambda i: i * 2 * s + s - 1)
            second = (row1 % (2 * s)) >= s
            w = jnp.exp(-jnp.abs(cum - mid))
            ql = jnp.where(second, q * w, 0.0).astype(BF16)
            kl = jnp.where(second, 0.0, k * w).astype(BF16)
            lev = _dot_nt(ql, kl)
            if 2 * s < c:
                lev = jnp.where(rowi // (2 * s) == coli // (2 * s), lev, 0.0)
            att = att + lev

        st = st_ref[h]
        o = _dot(att.astype(BF16), v) + _dot_nt((q * jnp.exp(cum)).astype(BF16), st.astype(BF16))
        last = cum[c - 1:c, :]
        kd = (k * jnp.exp(last - cum)).astype(BF16)
        st_ref[h] = st * jnp.exp(last) + _dot(v.T, kd)

        on = _rms(o, ng_ref[...])
        h_ref[0, rows, sl] = (on * _silu(g_ref[0, rows, sl])).astype(BF16)

    @pl.when(t == pl.num_programs(2) - 1)
    def _():
        for h in range(hb):
            sout_ref[0, h] = st_ref[h].T


def _hgrn_mix(proj, lb_logits, lb_row, norm_g, s0):
    b, t, _ = proj.shape
    c = min(t, A_CHUNK)
    nch = 4 if t % (4 * c) == 0 else 1
    tt = nch * c
    hb = 4
    nhb = A_HEADS // hb
    w = hb * A_D
    bs = s0.shape[0]
    col = lambda part: (lambda bi, hi, ti: (bi, ti, part * nhb + hi))
    kern = functools.partial(_hgrn_kernel, c=c, nch=nch, hb=hb, lb_row=lb_row)
    return pl.pallas_call(
        kern,
        grid=(b, nhb, t // tt),
        in_specs=[pl.BlockSpec((1, tt, w), col(0)), pl.BlockSpec((1, tt, w), col(1)),
                  pl.BlockSpec((1, tt, w), col(2)), pl.BlockSpec((1, tt, w), col(3)),
                  pl.BlockSpec((lb_logits.shape[0], w), lambda bi, hi, ti: (0, hi)),
                  pl.BlockSpec((1, A_D), lambda bi, hi, ti: (0, 0)),
                  pl.BlockSpec((1, hb, A_D, A_D), lambda bi, hi, ti: (bi if bs > 1 else 0, hi, 0, 0))],
        out_specs=[pl.BlockSpec((1, tt, w), lambda bi, hi, ti: (bi, ti, hi)),
                   pl.BlockSpec((1, hb, A_D, A_D), lambda bi, hi, ti: (bi, hi, 0, 0))],
        out_shape=[jax.ShapeDtypeStruct((b, t, E_WIDTH), BF16),
                   jax.ShapeDtypeStruct((b, A_HEADS, A_D, A_D), F32)],
        scratch_shapes=[pltpu.VMEM((hb, A_D, A_D), F32)],
        compiler_params=_params(("parallel", "parallel", "arbitrary")),
        name="hgrn_mix",
    )(proj, proj, proj, proj, lb_logits, norm_g.reshape(1, A_D), s0)


def _rglru_kernel(xb_ref, gb_ref, cw_ref, cb_ref, wa_ref, ba_ref, wx_ref, bx_ref, lam_ref, h0_ref, buf_ref,
                  h_ref, hl_ref, nbuf_ref, ext_s, a_s, b_s, hc_s, *, tt, reset_first):
    t = pl.program_id(1)
    hist = 8

    @pl.when(t == 0)
    def _():
        ext_s[hist - (B_CONV - 1):hist, :] = buf_ref[0]
        hc_s[...] = h0_ref[0]

    ext_s[hist:hist + tt, :] = xb_ref[0]
    xc = cb_ref[...] + cw_ref[0:1, :] * ext_s[hist - 3:hist - 3 + tt, :]
    for k in range(1, B_CONV):
        xc = xc + cw_ref[k:k + 1, :] * ext_s[hist - 3 + k:hist - 3 + k + tt, :]
    nbuf_ref[0] = ext_s[hist + tt - (B_CONV - 1):hist + tt, :]
    ext_s[hist - (B_CONV - 1):hist, :] = ext_s[hist + tt - (B_CONV - 1):hist + tt, :]

    lam = lam_ref[...]
    sp = jnp.maximum(-lam, 0.0) + jnp.log1p(jnp.exp(-jnp.abs(lam)))
    row = lax.broadcasted_iota(jnp.int32, (tt, B_BS), 0)
    for n in range(B_BLOCKS):
        sl = slice(n * B_BS, (n + 1) * B_BS)
        xn = xc[:, sl]
        xnb = xn.astype(BF16)
        r = _sigmoid(_dot(xnb, wa_ref[n]) + ba_ref[:, sl])
        ig = _sigmoid(_dot(xnb, wx_ref[n]) + bx_ref[:, sl])
        log_a = -B_C * r * sp[:, sl]
        a = jnp.exp(log_a)
        mult = jnp.sqrt(jnp.tanh(-log_a) * (1.0 + a * a))
        if reset_first:
            mult = jnp.where((row == 0) & (t == 0), 1.0, mult)
        a_s[:, sl] = a
        b_s[:, sl] = mult * ig * xn

    row8 = lax.broadcasted_iota(jnp.int32, (8, E_WIDTH), 0)

    def body(g, hc):
        r0 = pl.multiple_of(g * 8, 8)
        a = a_s[pl.ds(r0, 8), :]
        b = b_s[pl.ds(r0, 8), :]
        for s in (1, 2, 4):
            m = row8 >= s
            b = jnp.where(m, a * pltpu.roll(b, s, axis=0) + b, b)
            a = jnp.where(m, a * pltpu.roll(a, s, axis=0), a)
        hrows = a * hc + b
        b_s[pl.ds(r0, 8), :] = hrows
        return hrows[7:8, :]

    hc = lax.fori_loop(0, tt // 8, body, hc_s[...])
    hc_s[...] = hc
    hl_ref[0] = hc
    h_ref[0] = (b_s[...] * _silu(gb_ref[0])).astype(BF16)


def _rglru_mix(proj, conv_w, conv_b, wa, ba, wx, bx, lam, h0, buf, reset_first):
    b, t, _ = proj.shape
    tt = min(t, 256)
    bs = h0.shape[0]
    e = E_WIDTH
    row = lambda a: a.reshape(1, e)
    full = lambda shape: pl.BlockSpec(shape, lambda bi, ti: (0,) * len(shape))
    kern = functools.partial(_rglru_kernel, tt=tt, reset_first=reset_first)
    return pl.pallas_call(
        kern,
        grid=(b, t // tt),
        in_specs=[pl.BlockSpec((1, tt, e), lambda bi, ti: (bi, ti, 0)),
                  pl.BlockSpec((1, tt, e), lambda bi, ti: (bi, ti, 1)),
                  full((B_CONV, e)), full((1, e)),
                  full((B_BLOCKS, B_BS, B_BS)), full((1, e)),
                  full((B_BLOCKS, B_BS, B_BS)), full((1, e)), full((1, e)),
                  pl.BlockSpec((1, 1, e), lambda bi, ti: (bi if bs > 1 else 0, 0, 0)),
                  pl.BlockSpec((1, B_CONV - 1, e), lambda bi, ti: (bi if bs > 1 else 0, 0, 0))],
        out_specs=[pl.BlockSpec((1, tt, e), lambda bi, ti: (bi, ti, 0)),
                   pl.BlockSpec((1, 1, e), lambda bi, ti: (bi, 0, 0)),
                   pl.BlockSpec((1, B_CONV - 1, e), lambda bi, ti: (bi, 0, 0))],
        out_shape=[jax.ShapeDtypeStruct((b, t, e), BF16),
                   jax.ShapeDtypeStruct((b, 1, e), F32),
                   jax.ShapeDtypeStruct((b, B_CONV - 1, e), F32)],
        scratch_shapes=[pltpu.VMEM((8 + tt, e), F32), pltpu.VMEM((tt, e), F32), pltpu.VMEM((tt, e), F32),
                        pltpu.VMEM((1, e), F32)],
        compiler_params=_params(("parallel", "arbitrary")),
        name="rglru_mix",
    )(proj, proj, conv_w, row(conv_b), wa, row(ba), wx, row(bx), row(lam), h0.reshape(bs, 1, e), buf)


def _mla_q_kernel(ql_ref, kv_ref, cos_ref, sin_ref, qn_ref, kvn_ref, wn_ref, wp_ref, wps_ref, wuk_ref,
                  qa_ref, qp_ref, ckv_ref, kpe_ref):
    ql = _rms(ql_ref[0], qn_ref[...]).astype(BF16)
    q_nope = _dot(ql, wn_ref[...])
    q_pe = _dot(ql, wp_ref[...])
    q_sw = _dot(ql, wps_ref[...])
    cos = cos_ref[...]
    sin = sin_ref[...]
    for j in range(C_HEADS // 2):
        sl = slice(j * 128, (j + 1) * 128)
        r = (q_pe[:, sl] * cos + q_sw[:, sl] * sin).astype(BF16)
        qp_ref[0, 2 * j] = r[:, :C_ROPE]
        qp_ref[0, 2 * j + 1] = r[:, C_ROPE:]
    for h in range(C_HEADS):
        qh = q_nope[:, h * C_NOPE:(h + 1) * C_NOPE].astype(BF16)
        qa_ref[0, h] = _dot(qh, wuk_ref[h]).astype(BF16)
    kv = kv_ref[0]
    ckv_ref[0] = _rms(kv[:, :C_KV_LORA], kvn_ref[...])
    k_pe = kv[:, C_KV_LORA:C_KV_LORA + C_ROPE]
    k_sw = kv[:, C_KV_LORA + C_ROPE:C_KV_LORA + 2 * C_ROPE]
    kpe_ref[0] = k_pe * cos[:, :C_ROPE] + k_sw * sin[:, :C_ROPE]


def _mla_q(proj, cos, sin, q_norm, kv_norm, w_nope, w_pe, w_pe_sw, w_uk_h):
    b, t, _ = proj.shape
    tm = min(t, 256)
    full = lambda shape: pl.BlockSpec(shape, lambda bi, ti: (0,) * len(shape))
    return pl.pallas_call(
        _mla_q_kernel,
        grid=(b, t // tm),
        in_specs=[pl.BlockSpec((1, tm, C_Q_LORA), lambda bi, ti: (bi, ti, E_WIDTH // C_Q_LORA)),
                  pl.BlockSpec((1, tm, 512), lambda bi, ti: (bi, ti, (E_WIDTH + C_Q_LORA) // 512)),
                  pl.BlockSpec((tm, 128), lambda bi, ti: (ti, 0)),
                  pl.BlockSpec((tm, 128), lambda bi, ti: (ti, 0)),
                  full((1, C_Q_LORA)), full((1, C_KV_LORA)),
                  full((C_Q_LORA, C_HEADS * C_NOPE)), full((C_Q_LORA, C_HEADS * C_ROPE)),
                  full((C_Q_LORA, C_HEADS * C_ROPE)), full((C_HEADS, C_NOPE, C_KV_LORA))],
        out_specs=[pl.BlockSpec((1, C_HEADS, tm, C_KV_LORA), lambda bi, ti: (bi, 0, ti, 0)),
                   pl.BlockSpec((1, C_HEADS, tm, C_ROPE), lambda bi, ti: (bi, 0, ti, 0)),
                   pl.BlockSpec((1, tm, C_KV_LORA), lambda bi, ti: (bi, ti, 0)),
                   pl.BlockSpec((1, tm, C_ROPE), lambda bi, ti: (bi, ti, 0))],
        out_shape=[jax.ShapeDtypeStruct((b, C_HEADS, t, C_KV_LORA), BF16),
                   jax.ShapeDtypeStruct((b, C_HEADS, t, C_ROPE), BF16),
                   jax.ShapeDtypeStruct((b, t, C_KV_LORA), F32),
                   jax.ShapeDtypeStruct((b, t, C_ROPE), F32)],
        compiler_params=_params(("parallel", "arbitrary")),
        name="mla_q",
    )(proj, proj, cos, sin, q_norm.reshape(1, -1), kv_norm.reshape(1, -1), w_nope, w_pe, w_pe_sw, w_uk_h)


def _mla_attn_kernel(qa_ref, qp_ref, kc_ref, kct_ref, kp_ref, g_ref, wuv_ref, h_ref, acc_s, s_s, p_s, *,
                     tq, n_prefix, n_total, chunked):
    ci = pl.program_id(1)
    r = C_HEADS * tq
    nv = (n_prefix + tq * (ci + 1)) if chunked else n_total
    nblk = (nv + KEY_BLOCK - 1) // KEY_BLOCK
    qa = qa_ref[0].reshape(r, C_KV_LORA)
    qp = qp_ref[0].reshape(r, C_ROPE)
    acc_s[...] = jnp.zeros((C_KV_LORA, r), F32)
    c2 = C_SCALE * LOG2E

    def scores(j):
        off = pl.multiple_of(j * KEY_BLOCK, KEY_BLOCK)
        return _dot_nt(kc_ref[0, pl.ds(off, KEY_BLOCK), :], qa) + _dot_nt(kp_ref[0, pl.ds(off, KEY_BLOCK), :], qp)

    def add_pv(j, alpha):
        off = pl.multiple_of(j * KEY_BLOCK, KEY_BLOCK)
        acc_s[...] = alpha * acc_s[...] + _dot(kct_ref[0, :, pl.ds(off, KEY_BLOCK)], p_s[...])

    s_s[...] = scores(0)
    p_s[...] = jnp.zeros((KEY_BLOCK, r), BF16)

    def body(j, carry):
        m_old, l_old, alpha_prev = carry
        s = s_s[...]
        add_pv(jnp.maximum(j - 1, 0), alpha_prev)
        s_s[...] = scores(jnp.minimum(j + 1, nblk - 1))
        key = j * KEY_BLOCK + lax.broadcasted_iota(jnp.int32, (KEY_BLOCK, 1), 0)
        s = jnp.where(key < nv, s, -jnp.inf)
        m_new = jnp.maximum(m_old, jnp.max(s, axis=0, keepdims=True))
        alpha = jnp.exp2((m_old - m_new) * c2)
        p = jnp.exp2((s - m_new) * c2)
        l_new = alpha * l_old + jnp.sum(p, axis=0, keepdims=True)
        p_s[...] = p.astype(BF16)
        return m_new, l_new, alpha

    init = (jnp.full((1, r), -jnp.inf, F32), jnp.zeros((1, r), F32), jnp.ones((1, r), F32))
    _, l_fin, alpha_fin = lax.fori_loop(0, nblk, body, init)
    add_pv(nblk - 1, alpha_fin)
    o_lat = (acc_s[...] / l_fin).T.astype(BF16)
    for h in range(C_HEADS):
        oh = _dot(o_lat[h * tq:(h + 1) * tq], wuv_ref[h])
        sl = slice(h * C_V, (h + 1) * C_V)
        h_ref[0, :, sl] = (oh * _silu(g_ref[0, :, sl])).astype(BF16)


def _mla_attn(qa, qp, kc, kp, proj, w_uv_h, n_prefix, chunked):
    b, _, t, _ = qa.shape
    tq = min(t, CHUNK)
    n_total = n_prefix + t
    tk = kc.shape[1]
    r = C_HEADS * tq
    kct = jnp.swapaxes(kc, 1, 2)
    kern = functools.partial(_mla_attn_kernel, tq=tq, n_prefix=n_prefix, n_total=n_total, chunked=chunked)
    return pl.pallas_call(
        kern,
        grid=(b, t // tq),
        in_specs=[pl.BlockSpec((1, C_HEADS, tq, C_KV_LORA), lambda bi, ci: (bi, 0, ci, 0)),
                  pl.BlockSpec((1, C_HEADS, tq, C_ROPE), lambda bi, ci: (bi, 0, ci, 0)),
                  pl.BlockSpec((1, tk, C_KV_LORA), lambda bi, ci: (bi, 0, 0)),
                  pl.BlockSpec((1, C_KV_LORA, tk), lambda bi, ci: (bi, 0, 0)),
                  pl.BlockSpec((1, tk, C_ROPE), lambda bi, ci: (bi, 0, 0)),
                  pl.BlockSpec((1, tq, E_WIDTH), lambda bi, ci: (bi, ci, 0)),
                  pl.BlockSpec((C_HEADS, C_KV_LORA, C_V), lambda bi, ci: (0, 0, 0))],
        out_specs=pl.BlockSpec((1, tq, E_WIDTH), lambda bi, ci: (bi, ci, 0)),
        out_shape=jax.ShapeDtypeStruct((b, t, E_WIDTH), BF16),
        scratch_shapes=[pltpu.VMEM((C_KV_LORA, r), F32), pltpu.VMEM((KEY_BLOCK, r), F32),
                        pltpu.VMEM((KEY_BLOCK, r), BF16)],
        compiler_params=_params(("parallel", "arbitrary")),
        name="mla_attn",
    )(qa, qp, kc, kct, kp, proj, w_uv_h)


def _conf_kernel(a_ref, b_ref, g_ref, cw_ref, cb_ref, lg_ref, lb_ref, buf_ref, h_ref, nbuf_ref, ext_s, conv_s, sh_s,
                 *, tt):
    t = pl.program_id(1)
    nhist = D_CONV - 1
    lead = D_HIST - nhist

    @pl.when(t == 0)
    def _():
        ext_s[lead:D_HIST, :] = buf_ref[0]

    ext_s[D_HIST:D_HIST + tt, :] = a_ref[0] * _sigmoid(b_ref[0])

    cw = 256
    rc = min(tt, 64)
    nsh = tt + D_HIST - 8

    def col_body(ci, carry):
        c0 = pl.multiple_of(ci * cw, cw)
        for r in range(1, 8):
            sh_s[r - 1] = ext_s[r:r + nsh, pl.ds(c0, cw)]
        for r0 in range(0, tt, rc):
            acc = jnp.broadcast_to(cb_ref[:, pl.ds(c0, cw)], (rc, cw))
            for k in range(D_CONV):
                j8, r = divmod(lead + k, 8)
                lo = 8 * j8 + r0
                win = ext_s[lo:lo + rc, pl.ds(c0, cw)] if r == 0 else sh_s[r - 1, lo:lo + rc, :]
                acc = acc + cw_ref[k:k + 1, pl.ds(c0, cw)] * win
            conv_s[r0:r0 + rc, pl.ds(c0, cw)] = acc
        return carry

    lax.fori_loop(0, E_WIDTH // cw, col_body, 0)

    nbuf_ref[0] = ext_s[tt + lead:tt + D_HIST, :]
    ext_s[0:D_HIST, :] = ext_s[tt:tt + D_HIST, :]

    rn = 16

    def norm_body(i, carry):
        r0 = pl.multiple_of(i * rn, rn)
        c = conv_s[pl.ds(r0, rn), :]
        mu = jnp.mean(c, axis=-1, keepdims=True)
        xc = c - mu
        var = jnp.mean(xc * xc, axis=-1, keepdims=True)
        y = _silu(xc * lax.rsqrt(var + EPS) * lg_ref[...] + lb_ref[...])
        h_ref[0, pl.ds(r0, rn), :] = (y * _silu(g_ref[0, pl.ds(r0, rn), :])).astype(BF16)
        return carry

    lax.fori_loop(0, tt // rn, norm_body, 0, unroll=min(4, tt // rn))


def _conf_mix(proj, conv_w, conv_b, ln_g, ln_b, buf):
    b, t, _ = proj.shape
    tt = min(t, 256)
    bs = buf.shape[0]
    e = E_WIDTH
    row = lambda a: a.reshape(1, e)
    full = lambda shape: pl.BlockSpec(shape, lambda bi, ti: (0,) * len(shape))
    kern = functools.partial(_conf_kernel, tt=tt)
    return pl.pallas_call(
        kern,
        grid=(b, t // tt),
        in_specs=[pl.BlockSpec((1, tt, e), lambda bi, ti: (bi, ti, 0)),
                  pl.BlockSpec((1, tt, e), lambda bi, ti: (bi, ti, 1)),
                  pl.BlockSpec((1, tt, e), lambda bi, ti: (bi, ti, 2)),
                  full((D_CONV, e)), full((1, e)), full((1, e)), full((1, e)),
                  pl.BlockSpec((1, D_CONV - 1, e), lambda bi, ti: (bi if bs > 1 else 0, 0, 0))],
        out_specs=[pl.BlockSpec((1, tt, e), lambda bi, ti: (bi, ti, 0)),
                   pl.BlockSpec((1, D_CONV - 1, e), lambda bi, ti: (bi, 0, 0))],
        out_shape=[jax.ShapeDtypeStruct((b, t, e), BF16),
                   jax.ShapeDtypeStruct((b, D_CONV - 1, e), F32)],
        scratch_shapes=[pltpu.VMEM((D_HIST + tt, e), F32), pltpu.VMEM((tt, e), F32),
                        pltpu.VMEM((7, tt + D_HIST - 8, 256), F32)],
        compiler_params=_params(("parallel", "arbitrary")),
        name="conf_mix",
    )(proj, proj, proj, conv_w, row(conv_b), row(ln_g), row(ln_b), buf)


def _resident(shape):
    return pl.BlockSpec(shape, lambda *_: (0,) * len(shape), pipeline_mode=pl.Buffered(1))


def _hgrn_layer_kernel(x_ref, gpre_ref, wq_ref, wf_ref, wi_ref, wg_ref, lbl_ref, ng_ref, s0_ref, h_ref, sout_ref,
                       st_ref, pj_s, *, c, nch, hb, lb_row):
    t = pl.program_id(2)

    @pl.when(t == 0)
    def _():
        for h in range(hb):
            st_ref[h] = s0_ref[0, h].T

    lg = lbl_ref[...]
    e = jnp.exp(lg - jnp.max(lg, axis=0, keepdims=True))
    lb_all = jnp.sum(e[:lb_row + 1], axis=0, keepdims=True) / jnp.sum(e, axis=0, keepdims=True)

    rowi = lax.broadcasted_iota(jnp.int32, (c, c), 0)
    coli = lax.broadcasted_iota(jnp.int32, (c, c), 1)
    diag_mask = (rowi // A_SUB == coli // A_SUB) & (coli <= rowi)
    halves = []
    s = A_SUB
    while 2 * s <= c:
        halves.append(s)
        s *= 2
    row1 = lax.broadcasted_iota(jnp.int32, (c, A_D), 0)

    u = _rms(x_ref[0], gpre_ref[...]).astype(BF16)
    w_refs = (wq_ref, wf_ref, wi_ref, wg_ref)

    def project(hp, part):
        pj_s[hp, part] = _dot(u, w_refs[part][:, hp * 2 * A_D:(hp + 1) * 2 * A_D])

    for part in range(4):
        project(0, part)
    slots = [(ch, hh) for ch in range(nch) for hh in range(2)]
    per_slot = -(-4 // len(slots))
    for hp in range(hb // 2):
        for si, (ch, hh) in enumerate(slots):
            if hp + 1 < hb // 2:
                for part in range(si * per_slot, min(4, (si + 1) * per_slot)):
                    project(hp + 1, part)
            h = 2 * hp + hh
            sl = slice(h * A_D, (h + 1) * A_D)
            ln = slice(hh * A_D, (hh + 1) * A_D)
            rows = slice(ch * c, (ch + 1) * c)
            lb = lb_all[:, sl]
            q = _silu(pj_s[hp, 0, rows, ln])
            fg = lb + (1.0 - lb) * _sigmoid(pj_s[hp, 1, rows, ln])
            k = 1.0 - fg
            v = pj_s[hp, 2, rows, ln].astype(BF16)
            cum = _cumsum_rows(jnp.log(fg))

            start = _piecewise_rows(cum, A_SUB, lambda i: None if i == 0 else i * A_SUB - 1)
            rel = cum - start
            att = jnp.where(diag_mask,
                            _dot_nt((q * jnp.exp(rel)).astype(BF16), (k * jnp.exp(-rel)).astype(BF16)), 0.0)
            for s in halves:
                mid = _piecewise_rows(cum, 2 * s, lambda i: i * 2 * s + s - 1)
                second = (row1 % (2 * s)) >= s
                w = jnp.exp(-jnp.abs(cum - mid))
                ql = jnp.where(second, q * w, 0.0).astype(BF16)
                kl = jnp.where(second, 0.0, k * w).astype(BF16)
                lev = _dot_nt(ql, kl)
                if 2 * s < c:
                    lev = jnp.where(rowi // (2 * s) == coli // (2 * s), lev, 0.0)
                att = att + lev

            st = st_ref[h]
            o = _dot(att.astype(BF16), v) + _dot_nt((q * jnp.exp(cum)).astype(BF16), st.astype(BF16))
            last = cum[c - 1:c, :]
            kd = (k * jnp.exp(last - cum)).astype(BF16)
            st_ref[h] = st * jnp.exp(last) + _dot(v.T, kd)

            on = _rms(o, ng_ref[...])
            h_ref[0, rows, sl] = (on * _silu(pj_s[hp, 3, rows, ln])).astype(BF16)

    @pl.when(t == pl.num_programs(2) - 1)
    def _():
        for h in range(hb):
            sout_ref[0, h] = st_ref[h].T


def _hgrn_layer(x, g_pre, w_in, lb_logits, lb_row, norm_g, s0):
    b, t, d = x.shape
    c = min(t, A_CHUNK)
    nch = 2 if t % (2 * c) == 0 else 1
    tt = nch * c
    hb = 8
    nhb = A_HEADS // hb
    w = hb * A_D
    bs = s0.shape[0]
    wcol = lambda part: pl.BlockSpec((d, w), lambda bi, hi, ti: (0, part * nhb + hi))
    kern = functools.partial(_hgrn_layer_kernel, c=c, nch=nch, hb=hb, lb_row=lb_row)
    return pl.pallas_call(
        kern,
        grid=(b, nhb, t // tt),
        in_specs=[pl.BlockSpec((1, tt, d), lambda bi, hi, ti: (bi, ti, 0)),
                  pl.BlockSpec((1, d), lambda bi, hi, ti: (0, 0)),
                  wcol(0), wcol(1), wcol(2), wcol(3),
                  pl.BlockSpec((lb_logits.shape[0], w), lambda bi, hi, ti: (0, hi)),
                  pl.BlockSpec((1, A_D), lambda bi, hi, ti: (0, 0)),
                  pl.BlockSpec((1, hb, A_D, A_D), lambda bi, hi, ti: (bi if bs > 1 else 0, hi, 0, 0))],
        out_specs=[pl.BlockSpec((1, tt, w), lambda bi, hi, ti: (bi, ti, hi)),
                   pl.BlockSpec((1, hb, A_D, A_D), lambda bi, hi, ti: (bi, hi, 0, 0))],
        out_shape=[jax.ShapeDtypeStruct((b, t, E_WIDTH), BF16),
                   jax.ShapeDtypeStruct((b, A_HEADS, A_D, A_D), F32)],
        scratch_shapes=[pltpu.VMEM((hb, A_D, A_D), F32), pltpu.VMEM((hb // 2, 4, tt, 2 * A_D), F32)],
        compiler_params=_params(("parallel", "parallel", "arbitrary")),
        name="hgrn_layer",
    )(x, g_pre.reshape(1, d), w_in, w_in, w_in, w_in, lb_logits, norm_g.reshape(1, A_D), s0)


def _rglru_layer_kernel(x_ref, gpre_ref, win_ref, cw_ref, cb_ref, wa_ref, ba_ref, wx_ref, bx_ref, lam_ref, h0_ref,
                        buf_ref, h_ref, hl_ref, nbuf_ref, ext_s, a_s, b_s, gb_s, hc_s, *, tt, reset_first):
    t = pl.program_id(1)
    hist = 8
    nh = B_CONV - 1

    @pl.when(t == 0)
    def _():
        ext_s[hist - nh:hist, :] = buf_ref[0]
        hc_s[...] = h0_ref[0]

    u = _rms(x_ref[0], gpre_ref[...]).astype(BF16)
    lam = lam_ref[...]
    sp = jnp.maximum(-lam, 0.0) + jnp.log1p(jnp.exp(-jnp.abs(lam)))
    row = lax.broadcasted_iota(jnp.int32, (tt, B_BS), 0)
    def project_x(n):
        ext_s[hist:hist + tt, n * B_BS:(n + 1) * B_BS] = _dot(u, win_ref[:, n * B_BS:(n + 1) * B_BS])

    def project_g(n):
        gb_s[:, n * B_BS:(n + 1) * B_BS] = _dot(u, win_ref[:, E_WIDTH + n * B_BS:E_WIDTH + (n + 1) * B_BS])

    project_x(0)
    project_g(0)
    for n in range(B_BLOCKS):
        sl = slice(n * B_BS, (n + 1) * B_BS)
        xn = cb_ref[:, sl] + cw_ref[0:1, sl] * ext_s[hist - nh:hist - nh + tt, sl]
        for k in range(1, B_CONV):
            xn = xn + cw_ref[k:k + 1, sl] * ext_s[hist - nh + k:hist - nh + k + tt, sl]
        if n + 1 < B_BLOCKS:
            project_x(n + 1)
        xnb = xn.astype(BF16)
        r = _sigmoid(_dot(xnb, wa_ref[n]) + ba_ref[:, sl])
        if n + 1 < B_BLOCKS:
            project_g(n + 1)
        ig = _sigmoid(_dot(xnb, wx_ref[n]) + bx_ref[:, sl])
        log_a = -B_C * r * sp[:, sl]
        a = jnp.exp(log_a)
        mult = jnp.sqrt(jnp.tanh(-log_a) * (1.0 + a * a))
        if reset_first:
            mult = jnp.where((row == 0) & (t == 0), 1.0, mult)
        a_s[:, sl] = a
        b_s[:, sl] = mult * ig * xn

    tail = ext_s[hist + tt - nh:hist + tt, :]
    nbuf_ref[0] = tail
    ext_s[hist - nh:hist, :] = tail

    row8 = lax.broadcasted_iota(jnp.int32, (8, E_WIDTH), 0)

    def body(g, hc):
        r0 = pl.multiple_of(g * 8, 8)
        a = a_s[pl.ds(r0, 8), :]
        b = b_s[pl.ds(r0, 8), :]
        for s in (1, 2, 4):
            m = row8 >= s
            b = jnp.where(m, a * pltpu.roll(b, s, axis=0) + b, b)
            a = jnp.where(m, a * pltpu.roll(a, s, axis=0), a)
        hrows = a * hc + b
        h_ref[0, pl.ds(r0, 8), :] = (hrows * _silu(gb_s[pl.ds(r0, 8), :])).astype(BF16)
        return hrows[7:8, :]

    hc = lax.fori_loop(0, tt // 8, body, hc_s[...])
    hc_s[...] = hc
    hl_ref[0] = hc


def _rglru_layer(x, g_pre, w_in, conv_w, conv_b, wa, ba, wx, bx, lam, h0, buf, reset_first):
    b, t, d = x.shape
    tt = min(t, 256)
    bs = h0.shape[0]
    e = E_WIDTH
    row = lambda a: a.reshape(1, e)
    full = lambda shape: pl.BlockSpec(shape, lambda bi, ti: (0,) * len(shape))
    kern = functools.partial(_rglru_layer_kernel, tt=tt, reset_first=reset_first)
    return pl.pallas_call(
        kern,
        grid=(b, t // tt),
        in_specs=[pl.BlockSpec((1, tt, d), lambda bi, ti: (bi, ti, 0)),
                  full((1, d)), _resident((d, 2 * e)),
                  full((B_CONV, e)), full((1, e)),
                  full((B_BLOCKS, B_BS, B_BS)), full((1, e)),
                  full((B_BLOCKS, B_BS, B_BS)), full((1, e)), full((1, e)),
                  pl.BlockSpec((1, 1, e), lambda bi, ti: (bi if bs > 1 else 0, 0, 0)),
                  pl.BlockSpec((1, B_CONV - 1, e), lambda bi, ti: (bi if bs > 1 else 0, 0, 0))],
        out_specs=[pl.BlockSpec((1, tt, e), lambda bi, ti: (bi, ti, 0)),
                   pl.BlockSpec((1, 1, e), lambda bi, ti: (bi, 0, 0)),
                   pl.BlockSpec((1, B_CONV - 1, e), lambda bi, ti: (bi, 0, 0))],
        out_shape=[jax.ShapeDtypeStruct((b, t, e), BF16),
                   jax.ShapeDtypeStruct((b, 1, e), F32),
                   jax.ShapeDtypeStruct((b, B_CONV - 1, e), F32)],
        scratch_shapes=[pltpu.VMEM((8 + tt, e), F32), pltpu.VMEM((tt, e), F32), pltpu.VMEM((tt, e), F32),
                        pltpu.VMEM((tt, e), F32), pltpu.VMEM((1, e), F32)],
        compiler_params=_params(("parallel", "arbitrary")),
        name="rglru_layer",
    )(x, g_pre.reshape(1, d), w_in, conv_w, row(conv_b), wa, row(ba), wx, row(bx), row(lam), h0.reshape(bs, 1, e), buf)


def _conf_layer_kernel(x_ref, gpre_ref, win_ref, cw_ref, cb_ref, lg_ref, lb_ref, buf_ref, h_ref, nbuf_ref,
                       hist_s, win_s, sh_s, conv_s, gate_s, *, tt):
    t = pl.program_id(1)
    nhist = D_CONV - 1
    lead = D_HIST - nhist
    cw = 256
    nc = E_WIDTH // cw
    rc = min(tt, 64)
    nsh = tt + D_HIST - 8

    @pl.when(t == 0)
    def _():
        hist_s[0:lead, :] = jnp.zeros((lead, E_WIDTH), F32)
        hist_s[lead:D_HIST, :] = buf_ref[0]

    u = _rms(x_ref[0], gpre_ref[...]).astype(BF16)

    def project_stages(ci):
        cs = slice(ci * cw, (ci + 1) * cw)
        held = {}

        def glu_in():
            held["a"] = _dot(u, win_ref[:, cs])

        def glu_gate():
            b = _dot(u, win_ref[:, E_WIDTH + ci * cw:E_WIDTH + (ci + 1) * cw])
            win = win_s.at[ci % 2]
            win[0:D_HIST, :] = hist_s[:, cs]
            win[D_HIST:D_HIST + tt, :] = held["a"] * _sigmoid(b)

        def out_gate():
            gate_s[:, cs] = _dot(u, win_ref[:, 2 * E_WIDTH + ci * cw:2 * E_WIDTH + (ci + 1) * cw])

        return [glu_in, glu_gate, out_gate]

    for stage in project_stages(0):
        stage()
    nrc = tt // rc
    for ci in range(nc):
        pending = project_stages(ci + 1) if ci + 1 < nc else []
        per_slot = -(-len(pending) // nrc) if pending else 0
        cs = slice(ci * cw, (ci + 1) * cw)
        win = win_s.at[ci % 2]
        sh = sh_s.at[ci % 2]
        hist_s[:, cs] = win[tt:tt + D_HIST, :]
        for r in range(1, 8):
            sh[r - 1] = win[r:r + nsh, :]
        for ri, r0 in enumerate(range(0, tt, rc)):
            for stage in pending[ri * per_slot:(ri + 1) * per_slot]:
                stage()
            acc = jnp.broadcast_to(cb_ref[:, cs], (rc, cw))
            for k in range(D_CONV):
                j8, r = divmod(lead + k, 8)
                lo = 8 * j8 + r0
                taps = win[lo:lo + rc, :] if r == 0 else sh[r - 1, lo:lo + rc, :]
                acc = acc + cw_ref[k:k + 1, cs] * taps
            conv_s[r0:r0 + rc, cs] = acc

    nbuf_ref[0] = hist_s[lead:D_HIST, :]

    rn = 16

    def norm_body(i, carry):
        r0 = pl.multiple_of(i * rn, rn)
        c = conv_s[pl.ds(r0, rn), :]
        mu = jnp.mean(c, axis=-1, keepdims=True)
        xc = c - mu
        var = jnp.mean(xc * xc, axis=-1, keepdims=True)
        y = _silu(xc * lax.rsqrt(var + EPS) * lg_ref[...] + lb_ref[...])
        h_ref[0, pl.ds(r0, rn), :] = (y * _silu(gate_s[pl.ds(r0, rn), :])).astype(BF16)
        return carry

    lax.fori_loop(0, tt // rn, norm_body, 0, unroll=min(4, tt // rn))


def _conf_layer(x, g_pre, w_in, conv_w, conv_b, ln_g, ln_b, buf):
    b, t, d = x.shape
    tt = min(t, 256)
    bs = buf.shape[0]
    e = E_WIDTH
    row = lambda a: a.reshape(1, e)
    full = lambda shape: pl.BlockSpec(shape, lambda bi, ti: (0,) * len(shape))
    kern = functools.partial(_conf_layer_kernel, tt=tt)
    return pl.pallas_call(
        kern,
        grid=(b, t // tt),
        in_specs=[pl.BlockSpec((1, tt, d), lambda bi, ti: (bi, ti, 0)),
                  full((1, d)), _resident((d, 3 * e)),
                  full((D_CONV, e)), full((1, e)), full((1, e)), full((1, e)),
                  pl.BlockSpec((1, D_CONV - 1, e), lambda bi, ti: (bi if bs > 1 else 0, 0, 0))],
        out_specs=[pl.BlockSpec((1, tt, e), lambda bi, ti: (bi, ti, 0)),
                   pl.BlockSpec((1, D_CONV - 1, e), lambda bi, ti: (bi, 0, 0))],
        out_shape=[jax.ShapeDtypeStruct((b, t, e), BF16),
                   jax.ShapeDtypeStruct((b, D_CONV - 1, e), F32)],
        scratch_shapes=[pltpu.VMEM((D_HIST, e), F32), pltpu.VMEM((2, D_HIST + tt, 256), F32),
                        pltpu.VMEM((2, 7, tt + D_HIST - 8, 256), F32), pltpu.VMEM((tt, e), F32),
                        pltpu.VMEM((tt, e), F32)],
        compiler_params=_params(("parallel", "arbitrary")),
        name="conf_layer",
    )(x, g_pre.reshape(1, d), w_in, conv_w, row(conv_b), row(ln_g), row(ln_b), buf)


def _rope_tables(pos0, t):
    inv = 1.0 / (ROPE_BASE ** (jnp.arange(0, C_ROPE, 2, dtype=F32) / C_ROPE))
    ang = (pos0 + jnp.arange(t, dtype=jnp.int32)).astype(F32)[:, None] * inv[None, :]
    cos, sin = jnp.cos(ang), jnp.sin(ang)
    return jnp.tile(cos, (1, 4)), jnp.tile(jnp.concatenate([-sin, sin], axis=-1), (1, 2))


def _pad_keys(k):
    tk = k.shape[1]
    tkp = -(-tk // KEY_BLOCK) * KEY_BLOCK
    return jnp.pad(k.astype(BF16), ((0, 0), (0, tkp - tk), (0, 0)))


def _trunk(x, w, st, *, pos0, reset_first, chunked):
    b, t, d = x.shape
    m = b * t
    new = {}

    def layer(x, idx, mix):
        h = mix(x, w["norm_pre"][idx], w["w_in"][idx])
        return _out_res(h.reshape(m, E_WIDTH), w["w_out"][idx], w["norm_post"][idx], x.reshape(m, d)).reshape(b, t, d)

    def mix_a(x, g_pre, w_in):
        h, new["hgrn"] = _hgrn_layer(x, g_pre, w_in, w["a_lb_logits"], 0, w["a_norm_g"], st["hgrn"])
        return h

    def mix_b(x, g_pre, w_in):
        h, hl, new["rg_conv"] = _rglru_layer(x, g_pre, w_in, w["b_conv_w"], w["b_conv_b"], w["b_wa"], w["b_ba"],
                                             w["b_wx"], w["b_bx"], w["b_lambda"], st["rg_h"], st["rg_conv"],
                                             reset_first)
        new["rg_h"] = hl.reshape(b, E_WIDTH)
        return h

    def mix_c(x, g_pre, w_in):
        proj = _norm_proj(x.reshape(m, d), g_pre, w_in).reshape(b, t, -1)
        cos, sin = _rope_tables(pos0, t)
        qa, qp, ckv, kpe = _mla_q(proj, cos, sin, w["c_q_norm"], w["c_kv_norm"], w["c_w_nope"], w["c_w_pe"],
                                  w["c_w_pe_sw"], w["c_w_uk_h"])
        new["mla_c"], new["mla_pe"] = ckv, kpe
        kc, kp = ckv, kpe
        n_prefix = 0
        if st["mla_c"] is not None:
            pc, pp = st["mla_c"], st["mla_pe"]
            n_prefix = pc.shape[1]
            pc = jnp.broadcast_to(pc, (b,) + pc.shape[1:])
            pp = jnp.broadcast_to(pp, (b,) + pp.shape[1:])
            kc = jnp.concatenate([pc.astype(BF16), ckv.astype(BF16)], axis=1)
            kp = jnp.concatenate([pp.astype(BF16), kpe.astype(BF16)], axis=1)
        return _mla_attn(qa, qp, _pad_keys(kc), _pad_keys(kp), proj, w["c_w_uv_h"], n_prefix, chunked)

    def mix_d(x, g_pre, w_in):
        h, new["conf"] = _conf_layer(x, g_pre, w_in, w["d_conv_w"], w["d_conv_b"], w["d_ln_g"], w["d_ln_b"],
                                     st["conf"])
        return h

    for idx, mix in enumerate((mix_a, mix_b, mix_c, mix_d)):
        x = layer(x, idx, mix)
    return x, new


def kernel(x_prompt, x_sample, state_hgrn, state_rglru_h, state_rglru_conv, cache_mla_latent, cache_mla_rope, state_conformer_conv, meta_tokens, norm_pre, norm_post, a_w_in, a_lb_logits, a_norm_g, a_w_out, b_w_in, b_conv_w, b_conv_b, b_wa, b_ba, b_wx, b_bx, b_lambda, b_w_out, c_w_in, c_q_norm, c_kv_norm, c_w_uq, c_w_uk, c_w_uv, c_w_out, d_w_in, d_conv_w, d_conv_b, d_ln_g, d_ln_b, d_w_out):
    assert norm_pre.shape[0] == 4, "one layer of each mixer type"
    bf = lambda a: a.astype(BF16)

    c_in = c_w_in[0]
    i1, i2, i3 = C_Q_LORA, C_Q_LORA + C_KV_LORA, C_Q_LORA + C_KV_LORA + C_ROPE
    half = C_ROPE // 2
    k_pe_cols = c_in[:, i2:i3]
    k_pe_sw = jnp.concatenate([k_pe_cols[:, half:], k_pe_cols[:, :half]], axis=1)
    pad = jnp.zeros((D_MODEL, C_PROJ_W - (E_WIDTH + i3 + C_ROPE)), c_in.dtype)
    c_in_perm = jnp.concatenate([c_in[:, i3:], c_in[:, :i3], k_pe_sw, pad], axis=1)
    uq = c_w_uq[0].reshape(C_Q_LORA, C_HEADS, C_NOPE + C_ROPE)
    uq_pe = uq[:, :, C_NOPE:]
    uq_pe_sw = jnp.concatenate([uq_pe[..., half:], uq_pe[..., :half]], axis=-1)

    w = {
        "norm_pre": norm_pre, "norm_post": norm_post,
        "w_in": [bf(a_w_in[0]), bf(b_w_in[0]), bf(c_in_perm), bf(d_w_in[0])],
        "w_out": [bf(a_w_out[0]), bf(b_w_out[0]), bf(c_w_out[0]), bf(d_w_out[0])],
        "a_lb_logits": a_lb_logits, "a_norm_g": a_norm_g[0],
        "b_conv_w": b_conv_w[0], "b_conv_b": b_conv_b[0], "b_wa": bf(b_wa[0]), "b_ba": b_ba[0],
        "b_wx": bf(b_wx[0]), "b_bx": b_bx[0], "b_lambda": b_lambda[0],
        "c_q_norm": c_q_norm[0], "c_kv_norm": c_kv_norm[0],
        "c_w_nope": bf(uq[:, :, :C_NOPE].reshape(C_Q_LORA, C_HEADS * C_NOPE)),
        "c_w_pe": bf(uq_pe.reshape(C_Q_LORA, C_HEADS * C_ROPE)),
        "c_w_pe_sw": bf(uq_pe_sw.reshape(C_Q_LORA, C_HEADS * C_ROPE)),
        "c_w_uk_h": bf(jnp.transpose(c_w_uk[0].reshape(C_KV_LORA, C_HEADS, C_NOPE), (1, 2, 0))),
        "c_w_uv_h": bf(jnp.transpose(c_w_uv[0].reshape(C_KV_LORA, C_HEADS, C_V), (1, 0, 2))),
        "d_conv_w": d_conv_w[0], "d_conv_b": d_conv_b[0], "d_ln_g": d_ln_g[0], "d_ln_b": d_ln_b[0],
    }

    bp = x_prompt.shape[0]
    dt = x_prompt.dtype

    st_m = {"hgrn": jnp.zeros((1, A_HEADS, A_D, A_D), dt), "rg_h": jnp.zeros((1, E_WIDTH), dt),
            "rg_conv": jnp.zeros((1, B_CONV - 1, E_WIDTH), dt), "mla_c": None, "mla_pe": None,
            "conf": jnp.zeros((1, D_CONV - 1, E_WIDTH), dt)}
    _, new_m = _trunk(meta_tokens.astype(dt)[None], w, st_m, pos0=0, reset_first=True, chunked=False)

    st_p = {"hgrn": new_m["hgrn"], "rg_h": new_m["rg_h"], "rg_conv": new_m["rg_conv"],
            "mla_c": new_m["mla_c"], "mla_pe": new_m["mla_pe"], "conf": new_m["conf"]}
    yp, new_p = _trunk(x_prompt, w, st_p, pos0=N_META, reset_first=False, chunked=True)

    st_s = {"hgrn": state_hgrn[0], "rg_h": state_rglru_h[0], "rg_conv": state_rglru_conv[0],
            "mla_c": cache_mla_latent[0], "mla_pe": cache_mla_rope[0], "conf": state_conformer_conv[0]}
    ys, new_s = _trunk(x_sample, w, st_s, pos0=cache_mla_latent.shape[2], reset_first=False, chunked=False)

    def with_meta(meta_rows, rows):
        return jnp.concatenate([jnp.broadcast_to(meta_rows, (bp,) + meta_rows.shape[1:]), rows], axis=1)

    return (yp, ys,
            new_p["hgrn"][None], new_s["hgrn"][None],
            new_p["rg_h"][None], new_s["rg_h"][None],
            new_p["rg_conv"][None], new_s["rg_conv"][None],
            with_meta(new_m["mla_c"], new_p["mla_c"])[None], new_s["mla_c"][None],
            with_meta(new_m["mla_pe"], new_p["mla_pe"])[None], new_s["mla_pe"][None],
            new_p["conf"][None], new_s["conf"][None])
```

```python
import functools

import jax
import jax.numpy as jnp
from jax import lax
from jax.experimental import pallas as pl
from jax.experimental.pallas import tpu as pltpu

F32 = jnp.float32
BF16 = jnp.bfloat16

EPS = 1e-6
D_MODEL = 1024
E_WIDTH = 2048
N_META = 16
CHUNK = 64
A_HEADS = 16
A_D = 128
A_SUB = 16
A_CHUNK = 128
B_BLOCKS = 8
B_BS = 256
B_CONV = 4
B_C = 8.0
C_HEADS = 16
C_NOPE = 128
C_ROPE = 64
C_V = 128
C_Q_LORA = 512
C_KV_LORA = 256
C_SCALE = (C_NOPE + C_ROPE) ** -0.5
LOG2E = 1.4426950408889634
ROPE_BASE = 10000.0
KEY_BLOCK = 256
ATTN_COLS = 1024
D_CONV = 31
D_HIST = 32

VMEM_LIMIT = 48 * 1024 * 1024


def _params(sem):
    return pltpu.CompilerParams(dimension_semantics=sem, vmem_limit_bytes=VMEM_LIMIT)


def _dot(a, b):
    return jnp.dot(a, b, preferred_element_type=F32)


def _dot_nt(a, b):
    return lax.dot_general(a, b, (((1,), (1,)), ((), ())), preferred_element_type=F32)


def _sigmoid(x):
    return jax.nn.sigmoid(x)


def _silu(x):
    return x * jax.nn.sigmoid(x)


def _rms(x, g):
    return x * lax.rsqrt(jnp.mean(x * x, axis=-1, keepdims=True) + EPS) * g


def _out_res_kernel(h_ref, w_ref, g_ref, x_ref, o_ref):
    y = _dot(h_ref[...], w_ref[...])
    o_ref[...] = x_ref[...] + _rms(y, g_ref[...])


def _out_res(h2d, w, g, x2d):
    m, e = h2d.shape
    d = w.shape[1]
    tm = min(m, 1024)
    return pl.pallas_call(
        _out_res_kernel,
        grid=(m // tm,),
        in_specs=[pl.BlockSpec((tm, e), lambda i: (i, 0)),
                  pl.BlockSpec((e, d), lambda i: (0, 0)),
                  pl.BlockSpec((1, d), lambda i: (0, 0)),
                  pl.BlockSpec((tm, d), lambda i: (i, 0))],
        out_specs=pl.BlockSpec((tm, d), lambda i: (i, 0)),
        out_shape=jax.ShapeDtypeStruct((m, d), F32),
        compiler_params=_params(("parallel",)),
        name="out_res",
    )(h2d, w, g.reshape(1, d), x2d)


def _cumsum_rows(x):
    n = x.shape[0]
    row = lax.broadcasted_iota(jnp.int32, x.shape, 0)
    s = 1
    while s < n:
        x = x + jnp.where(row >= s, pltpu.roll(x, s, axis=0), 0.0)
        s *= 2
    return x


def _piecewise_rows(cum, seg, pick):
    c, l = cum.shape
    pieces = []
    for i in range(c // seg):
        r = pick(i)
        if r is None:
            pieces.append(jnp.zeros((seg, l), F32))
        else:
            pieces.append(jnp.broadcast_to(cum[r:r + 1, :], (seg, l)))
    return pieces[0] if len(pieces) == 1 else jnp.concatenate(pieces, axis=0)


def _mla_q_kernel(x_ref, gpre_ref, win_ref, cos_ref, sin_ref, qn_ref, kvn_ref, wn_ref, wp_ref, wps_ref, wuk_ref,
                  g_ref, qa_ref, qp_ref, ckv_ref, kpe_ref):
    u = _rms(x_ref[0], gpre_ref[...]).astype(BF16)
    g_ref[0] = _dot(u, win_ref[:, :E_WIDTH])
    q_lat = _dot(u, win_ref[:, E_WIDTH:E_WIDTH + C_Q_LORA])
    kv = _dot(u, win_ref[:, E_WIDTH + C_Q_LORA:])
    ql = _rms(q_lat, qn_ref[...]).astype(BF16)
    q_nope = _dot(ql, wn_ref[...])
    q_pe = _dot(ql, wp_ref[...])
    q_sw = _dot(ql, wps_ref[...])
    cos = cos_ref[...]
    sin = sin_ref[...]
    for j in range(C_HEADS // 2):
        sl = slice(j * 128, (j + 1) * 128)
        r = (q_pe[:, sl] * cos + q_sw[:, sl] * sin).astype(BF16)
        qp_ref[0, 2 * j] = r[:, :C_ROPE]
        qp_ref[0, 2 * j + 1] = r[:, C_ROPE:]
    for h in range(C_HEADS):
        qh = q_nope[:, h * C_NOPE:(h + 1) * C_NOPE].astype(BF16)
        qa_ref[0, h] = _dot(qh, wuk_ref[h]).astype(BF16)
    ckv_ref[0] = _rms(kv[:, :C_KV_LORA], kvn_ref[...])
    k_pe = kv[:, C_KV_LORA:C_KV_LORA + C_ROPE]
    k_sw = kv[:, C_KV_LORA + C_ROPE:C_KV_LORA + 2 * C_ROPE]
    kpe_ref[0] = k_pe * cos[:, :C_ROPE] + k_sw * sin[:, :C_ROPE]


def _mla_q(x, g_pre, w_in, cos, sin, q_norm, kv_norm, w_nope, w_pe, w_pe_sw, w_uk_h):
    b, t, d = x.shape
    tm = min(t, 256)
    full = lambda shape: pl.BlockSpec(shape, lambda bi, ti: (0,) * len(shape))
    return pl.pallas_call(
        _mla_q_kernel,
        grid=(b, t // tm),
        in_specs=[pl.BlockSpec((1, tm, d), lambda bi, ti: (bi, ti, 0)),
                  full((1, d)), _resident(w_in.shape),
                  pl.BlockSpec((tm, 128), lambda bi, ti: (ti, 0)),
                  pl.BlockSpec((tm, 128), lambda bi, ti: (ti, 0)),
                  full((1, C_Q_LORA)), full((1, C_KV_LORA)),
                  full((C_Q_LORA, C_HEADS * C_NOPE)), full((C_Q_LORA, C_HEADS * C_ROPE)),
                  full((C_Q_LORA, C_HEADS * C_ROPE)), full((C_HEADS, C_NOPE, C_KV_LORA))],
        out_specs=[pl.BlockSpec((1, tm, E_WIDTH), lambda bi, ti: (bi, ti, 0)),
                   pl.BlockSpec((1, C_HEADS, tm, C_KV_LORA), lambda bi, ti: (bi, 0, ti, 0)),
                   pl.BlockSpec((1, C_HEADS, tm, C_ROPE), lambda bi, ti: (bi, 0, ti, 0)),
                   pl.BlockSpec((1, tm, C_KV_LORA), lambda bi, ti: (bi, ti, 0)),
                   pl.BlockSpec((1, tm, C_ROPE), lambda bi, ti: (bi, ti, 0))],
        out_shape=[jax.ShapeDtypeStruct((b, t, E_WIDTH), F32),
                   jax.ShapeDtypeStruct((b, C_HEADS, t, C_KV_LORA), BF16),
                   jax.ShapeDtypeStruct((b, C_HEADS, t, C_ROPE), BF16),
                   jax.ShapeDtypeStruct((b, t, C_KV_LORA), F32),
                   jax.ShapeDtypeStruct((b, t, C_ROPE), F32)],
        compiler_params=_params(("parallel", "arbitrary")),
        name="mla_q",
    )(x, g_pre.reshape(1, d), w_in, cos, sin, q_norm.reshape(1, -1), kv_norm.reshape(1, -1), w_nope, w_pe, w_pe_sw,
      w_uk_h)


def _mla_attn_kernel(qa_ref, qp_ref, kc_ref, kct_ref, kp_ref, g_ref, wuv_ref, h_ref, acc_s, s_s, p_s, *,
                     tq, n_prefix, n_total, chunked):
    ci = pl.program_id(1)
    r = C_HEADS * tq
    nv = (n_prefix + tq * (ci + 1)) if chunked else n_total
    nblk = (nv + KEY_BLOCK - 1) // KEY_BLOCK
    qa = qa_ref[0].reshape(r, C_KV_LORA)
    qp = qp_ref[0].reshape(r, C_ROPE)
    acc_s[...] = jnp.zeros((C_KV_LORA, r), F32)
    c2 = C_SCALE * LOG2E

    ng = max(1, r // ATTN_COLS)
    groups = [slice(i * (r // ng), (i + 1) * (r // ng)) for i in range(ng)]

    def scores(j, cs, qa=qa, qp=qp):
        off = pl.multiple_of(j * KEY_BLOCK, KEY_BLOCK)
        return (_dot_nt(kc_ref[0, pl.ds(off, KEY_BLOCK), :], qa[cs]) +
                _dot_nt(kp_ref[0, pl.ds(off, KEY_BLOCK), :], qp[cs]))

    def add_pv(j, cs, alpha):
        off = pl.multiple_of(j * KEY_BLOCK, KEY_BLOCK)
        acc_s[:, cs] = alpha * acc_s[:, cs] + _dot(kct_ref[0, :, pl.ds(off, KEY_BLOCK)], p_s[:, cs])

    for cs in groups:
        s_s[:, cs] = scores(0, cs)
    p_s[...] = jnp.zeros((KEY_BLOCK, r), BF16)

    def body(j, carry):
        key = j * KEY_BLOCK + lax.broadcasted_iota(jnp.int32, (KEY_BLOCK, 1), 0)
        out = []
        for cs, (m_old, l_old, alpha_prev) in zip(groups, carry):
            s = s_s[:, cs]
            add_pv(jnp.maximum(j - 1, 0), cs, alpha_prev)
            s_s[:, cs] = scores(jnp.minimum(j + 1, nblk - 1), cs)
            s = jnp.where(key < nv, s, -jnp.inf)
            m_new = jnp.maximum(m_old, jnp.max(s, axis=0, keepdims=True))
            alpha = jnp.exp2((m_old - m_new) * c2)
            p = jnp.exp2((s - m_new) * c2)
            l_new = alpha * l_old + jnp.sum(p, axis=0, keepdims=True)
            p_s[:, cs] = p.astype(BF16)
            out.append((m_new, l_new, alpha))
        return tuple(out)

    w = r // ng
    init = tuple((jnp.full((1, w), -jnp.inf, F32), jnp.zeros((1, w), F32), jnp.ones((1, w), F32)) for _ in groups)
    fin = lax.fori_loop(0, nblk, body, init)
    for cs, (_, _, alpha_fin) in zip(groups, fin):
        add_pv(nblk - 1, cs, alpha_fin)
    l_fin = jnp.concatenate([f[1] for f in fin], axis=1) if ng > 1 else fin[0][1]
    o_lat = (acc_s[...] / l_fin).T.astype(BF16)
    for h in range(C_HEADS):
        oh = _dot(o_lat[h * tq:(h + 1) * tq], wuv_ref[h])
        sl = slice(h * C_V, (h + 1) * C_V)
        h_ref[0, :, sl] = (oh * _silu(g_ref[0, :, sl])).astype(BF16)


def _mla_attn(qa, qp, kc, kp, proj, w_uv_h, n_prefix, chunked):
    b, _, t, _ = qa.shape
    tq = min(t, CHUNK)
    n_total = n_prefix + t
    tk = kc.shape[1]
    r = C_HEADS * tq
    kct = jnp.swapaxes(kc, 1, 2)
    kern = functools.partial(_mla_attn_kernel, tq=tq, n_prefix=n_prefix, n_total=n_total, chunked=chunked)
    return pl.pallas_call(
        kern,
        grid=(b, t // tq),
        in_specs=[pl.BlockSpec((1, C_HEADS, tq, C_KV_LORA), lambda bi, ci: (bi, 0, ci, 0)),
                  pl.BlockSpec((1, C_HEADS, tq, C_ROPE), lambda bi, ci: (bi, 0, ci, 0)),
                  pl.BlockSpec((1, tk, C_KV_LORA), lambda bi, ci: (bi, 0, 0)),
                  pl.BlockSpec((1, C_KV_LORA, tk), lambda bi, ci: (bi, 0, 0)),
                  pl.BlockSpec((1, tk, C_ROPE), lambda bi, ci: (bi, 0, 0)),
                  pl.BlockSpec((1, tq, E_WIDTH), lambda bi, ci: (bi, ci, 0)),
                  pl.BlockSpec((C_HEADS, C_KV_LORA, C_V), lambda bi, ci: (0, 0, 0))],
        out_specs=pl.BlockSpec((1, tq, E_WIDTH), lambda bi, ci: (bi, ci, 0)),
        out_shape=jax.ShapeDtypeStruct((b, t, E_WIDTH), BF16),
        scratch_shapes=[pltpu.VMEM((C_KV_LORA, r), F32), pltpu.VMEM((KEY_BLOCK, r), F32),
                        pltpu.VMEM((KEY_BLOCK, r), BF16)],
        compiler_params=_params(("parallel", "arbitrary")),
        name="mla_attn",
    )(qa, qp, kc, kct, kp, proj, w_uv_h)


def _resident(shape):
    return pl.BlockSpec(shape, lambda *_: (0,) * len(shape), pipeline_mode=pl.Buffered(1))


def _hgrn_layer_kernel(x_ref, gpre_ref, wq_ref, wf_ref, wi_ref, wg_ref, lbl_ref, ng_ref, s0_ref, h_ref, sout_ref,
                       st_ref, pj_s, *, c, nch, hb, lb_row):
    t = pl.program_id(2)

    @pl.when(t == 0)
    def _():
        for h in range(hb):
            st_ref[h] = s0_ref[0, h].T

    lg = lbl_ref[...]
    e = jnp.exp(lg - jnp.max(lg, axis=0, keepdims=True))
    lb_all = jnp.sum(e[:lb_row + 1], axis=0, keepdims=True) / jnp.sum(e, axis=0, keepdims=True)

    rowi = lax.broadcasted_iota(jnp.int32, (c, c), 0)
    coli = lax.broadcasted_iota(jnp.int32, (c, c), 1)
    diag_mask = (rowi // A_SUB == coli // A_SUB) & (coli <= rowi)
    halves = []
    s = A_SUB
    while 2 * s <= c:
        halves.append(s)
        s *= 2
    row1 = lax.broadcasted_iota(jnp.int32, (c, A_D), 0)

    u = _rms(x_ref[0], gpre_ref[...]).astype(BF16)
    w_refs = (wq_ref, wf_ref, wi_ref, wg_ref)

    def project(hp, part):
        pj_s[hp, part] = _dot(u, w_refs[part][:, hp * 2 * A_D:(hp + 1) * 2 * A_D])

    for part in range(4):
        project(0, part)
    slots = [(ch, hh) for ch in range(nch) for hh in range(2)]
    per_slot = -(-4 // len(slots))
    for hp in range(hb // 2):
        for si, (ch, hh) in enumerate(slots):
            if hp + 1 < hb // 2:
                for part in range(si * per_slot, min(4, (si + 1) * per_slot)):
                    project(hp + 1, part)
            h = 2 * hp + hh
            sl = slice(h * A_D, (h + 1) * A_D)
            ln = slice(hh * A_D, (hh + 1) * A_D)
            rows = slice(ch * c, (ch + 1) * c)
            lb = lb_all[:, sl]
            q = _silu(pj_s[hp, 0, rows, ln])
            fg = lb + (1.0 - lb) * _sigmoid(pj_s[hp, 1, rows, ln])
            k = 1.0 - fg
            v = pj_s[hp, 2, rows, ln].astype(BF16)
            cum = _cumsum_rows(jnp.log(fg))

            start = _piecewise_rows(cum, A_SUB, lambda i: None if i == 0 else i * A_SUB - 1)
            rel = cum - start
            att = jnp.where(diag_mask,
                            _dot_nt((q * jnp.exp(rel)).astype(BF16), (k * jnp.exp(-rel)).astype(BF16)), 0.0)
            for s in halves:
                mid = _piecewise_rows(cum, 2 * s, lambda i: i * 2 * s + s - 1)
                second = (row1 % (2 * s)) >= s
                w = jnp.exp(-jnp.abs(cum - mid))
                ql = jnp.where(second, q * w, 0.0).astype(BF16)
                kl = jnp.where(second, 0.0, k * w).astype(BF16)
                lev = _dot_nt(ql, kl)
                if 2 * s < c:
                    lev = jnp.where(rowi // (2 * s) == coli // (2 * s), lev, 0.0)
                att = att + lev

            st = st_ref[h]
            o = _dot(att.astype(BF16), v) + _dot_nt((q * jnp.exp(cum)).astype(BF16), st.astype(BF16))
            last = cum[c - 1:c, :]
            kd = (k * jnp.exp(last - cum)).astype(BF16)
            st_ref[h] = st * jnp.exp(last) + _dot(v.T, kd)

            on = _rms(o, ng_ref[...])
            h_ref[0, rows, sl] = (on * _silu(pj_s[hp, 3, rows, ln])).astype(BF16)

    @pl.when(t == pl.num_programs(2) - 1)
    def _():
        for h in range(hb):
            sout_ref[0, h] = st_ref[h].T


def _hgrn_layer(x, g_pre, w_in, lb_logits, lb_row, norm_g, s0):
    b, t, d = x.shape
    c = min(t, A_CHUNK)
    nch = 4 if t % (4 * c) == 0 else 1
    tt = nch * c
    hb = 8
    nhb = A_HEADS // hb
    w = hb * A_D
    bs = s0.shape[0]
    wcol = lambda part: pl.BlockSpec((d, w), lambda hi, bi, ti: (0, part * nhb + hi))
    kern = functools.partial(_hgrn_layer_kernel, c=c, nch=nch, hb=hb, lb_row=lb_row)
    return pl.pallas_call(
        kern,
        grid=(nhb, b, t // tt),
        in_specs=[pl.BlockSpec((1, tt, d), lambda hi, bi, ti: (bi, ti, 0)),
                  pl.BlockSpec((1, d), lambda hi, bi, ti: (0, 0)),
                  wcol(0), wcol(1), wcol(2), wcol(3),
                  pl.BlockSpec((lb_logits.shape[0], w), lambda hi, bi, ti: (0, hi)),
                  pl.BlockSpec((1, A_D), lambda hi, bi, ti: (0, 0)),
                  pl.BlockSpec((1, hb, A_D, A_D), lambda hi, bi, ti: (bi if bs > 1 else 0, hi, 0, 0))],
        out_specs=[pl.BlockSpec((1, tt, w), lambda hi, bi, ti: (bi, ti, hi)),
                   pl.BlockSpec((1, hb, A_D, A_D), lambda hi, bi, ti: (bi, hi, 0, 0))],
        out_shape=[jax.ShapeDtypeStruct((b, t, E_WIDTH), BF16),
                   jax.ShapeDtypeStruct((b, A_HEADS, A_D, A_D), F32)],
        scratch_shapes=[pltpu.VMEM((hb, A_D, A_D), F32), pltpu.VMEM((hb // 2, 4, tt, 2 * A_D), F32)],
        compiler_params=_params(("parallel", "parallel", "arbitrary")),
        name="hgrn_layer",
    )(x, g_pre.reshape(1, d), w_in, w_in, w_in, w_in, lb_logits, norm_g.reshape(1, A_D), s0)


def _rglru_layer_kernel(x_ref, gpre_ref, win_ref, cw_ref, cb_ref, wa_ref, ba_ref, wx_ref, bx_ref, lam_ref, h0_ref,
                        buf_ref, h_ref, hl_ref, nbuf_ref, ext_s, a_s, b_s, gb_s, hc_s, *, tt, reset_first):
    t = pl.program_id(1)
    hist = 8
    nh = B_CONV - 1

    @pl.when(t == 0)
    def _():
        ext_s[hist - nh:hist, :] = buf_ref[0]
        hc_s[...] = h0_ref[0]

    u = _rms(x_ref[0], gpre_ref[...]).astype(BF16)
    lam = lam_ref[...]
    sp = jnp.maximum(-lam, 0.0) + jnp.log1p(jnp.exp(-jnp.abs(lam)))
    row = lax.broadcasted_iota(jnp.int32, (tt, B_BS), 0)
    def project_x(n):
        ext_s[hist:hist + tt, n * B_BS:(n + 1) * B_BS] = _dot(u, win_ref[:, n * B_BS:(n + 1) * B_BS])

    def project_g(n):
        gb_s[:, n * B_BS:(n + 1) * B_BS] = _dot(u, win_ref[:, E_WIDTH + n * B_BS:E_WIDTH + (n + 1) * B_BS])

    project_x(0)
    project_g(0)
    for n in range(B_BLOCKS):
        sl = slice(n * B_BS, (n + 1) * B_BS)
        xn = cb_ref[:, sl] + cw_ref[0:1, sl] * ext_s[hist - nh:hist - nh + tt, sl]
        for k in range(1, B_CONV):
            xn = xn + cw_ref[k:k + 1, sl] * ext_s[hist - nh + k:hist - nh + k + tt, sl]
        if n + 1 < B_BLOCKS:
            project_x(n + 1)
        xnb = xn.astype(BF16)
        r = _sigmoid(_dot(xnb, wa_ref[n]) + ba_ref[:, sl])
        if n + 1 < B_BLOCKS:
            project_g(n + 1)
        ig = _sigmoid(_dot(xnb, wx_ref[n]) + bx_ref[:, sl])
        log_a = -B_C * r * sp[:, sl]
        a = jnp.exp(log_a)
        mult = jnp.sqrt(jnp.tanh(-log_a) * (1.0 + a * a))
        if reset_first:
            mult = jnp.where((row == 0) & (t == 0), 1.0, mult)
        a_s[:, sl] = a
        b_s[:, sl] = mult * ig * xn

    tail = ext_s[hist + tt - nh:hist + tt, :]
    nbuf_ref[0] = tail
    ext_s[hist - nh:hist, :] = tail

    row8 = lax.broadcasted_iota(jnp.int32, (8, E_WIDTH), 0)

    def body(g, hc):
        r0 = pl.multiple_of(g * 8, 8)
        a = a_s[pl.ds(r0, 8), :]
        b = b_s[pl.ds(r0, 8), :]
        for s in (1, 2, 4):
            m = row8 >= s
            b = jnp.where(m, a * pltpu.roll(b, s, axis=0) + b, b)
            a = jnp.where(m, a * pltpu.roll(a, s, axis=0), a)
        hrows = a * hc + b
        h_ref[0, pl.ds(r0, 8), :] = (hrows * _silu(gb_s[pl.ds(r0, 8), :])).astype(BF16)
        return hrows[7:8, :]

    hc = lax.fori_loop(0, tt // 8, body, hc_s[...])
    hc_s[...] = hc
    hl_ref[0] = hc


def _rglru_layer(x, g_pre, w_in, conv_w, conv_b, wa, ba, wx, bx, lam, h0, buf, reset_first):
    b, t, d = x.shape
    tt = min(t, 256)
    bs = h0.shape[0]
    e = E_WIDTH
    row = lambda a: a.reshape(1, e)
    full = lambda shape: pl.BlockSpec(shape, lambda bi, ti: (0,) * len(shape))
    kern = functools.partial(_rglru_layer_kernel, tt=tt, reset_first=reset_first)
    return pl.pallas_call(
        kern,
        grid=(b, t // tt),
        in_specs=[pl.BlockSpec((1, tt, d), lambda bi, ti: (bi, ti, 0)),
                  full((1, d)), _resident((d, 2 * e)),
                  full((B_CONV, e)), full((1, e)),
                  full((B_BLOCKS, B_BS, B_BS)), full((1, e)),
                  full((B_BLOCKS, B_BS, B_BS)), full((1, e)), full((1, e)),
                  pl.BlockSpec((1, 1, e), lambda bi, ti: (bi if bs > 1 else 0, 0, 0)),
                  pl.BlockSpec((1, B_CONV - 1, e), lambda bi, ti: (bi if bs > 1 else 0, 0, 0))],
        out_specs=[pl.BlockSpec((1, tt, e), lambda bi, ti: (bi, ti, 0)),
                   pl.BlockSpec((1, 1, e), lambda bi, ti: (bi, 0, 0)),
                   pl.BlockSpec((1, B_CONV - 1, e), lambda bi, ti: (bi, 0, 0))],
        out_shape=[jax.ShapeDtypeStruct((b, t, e), BF16),
                   jax.ShapeDtypeStruct((b, 1, e), F32),
                   jax.ShapeDtypeStruct((b, B_CONV - 1, e), F32)],
        scratch_shapes=[pltpu.VMEM((8 + tt, e), F32), pltpu.VMEM((tt, e), F32), pltpu.VMEM((tt, e), F32),
                        pltpu.VMEM((tt, e), F32), pltpu.VMEM((1, e), F32)],
        compiler_params=_params(("parallel", "arbitrary")),
        name="rglru_layer",
    )(x, g_pre.reshape(1, d), w_in, conv_w, row(conv_b), wa, row(ba), wx, row(bx), row(lam), h0.reshape(bs, 1, e), buf)


def _conf_layer_kernel(x_ref, gpre_ref, win_ref, cw_ref, cb_ref, lg_ref, lb_ref, buf_ref, h_ref, nbuf_ref,
                       hist_s, win_s, sh_s, conv_s, gate_s, *, tt):
    t = pl.program_id(1)
    nhist = D_CONV - 1
    lead = D_HIST - nhist
    cw = 256
    nc = E_WIDTH // cw
    rc = min(tt, 64)
    nsh = tt + D_HIST - 8

    @pl.when(t == 0)
    def _():
        hist_s[0:lead, :] = jnp.zeros((lead, E_WIDTH), F32)
        hist_s[lead:D_HIST, :] = buf_ref[0]

    u = _rms(x_ref[0], gpre_ref[...]).astype(BF16)

    def project_stages(ci):
        cs = slice(ci * cw, (ci + 1) * cw)
        held = {}

        def glu_in():
            held["a"] = _dot(u, win_ref[:, cs])

        def glu_gate():
            b = _dot(u, win_ref[:, E_WIDTH + ci * cw:E_WIDTH + (ci + 1) * cw])
            win = win_s.at[ci % 2]
            win[0:D_HIST, :] = hist_s[:, cs]
            win[D_HIST:D_HIST + tt, :] = held["a"] * _sigmoid(b)

        def out_gate():
            gate_s[:, cs] = _dot(u, win_ref[:, 2 * E_WIDTH + ci * cw:2 * E_WIDTH + (ci + 1) * cw])

        return [glu_in, glu_gate, out_gate]

    for stage in project_stages(0):
        stage()
    nrc = tt // rc
    for ci in range(nc):
        pending = project_stages(ci + 1) if ci + 1 < nc else []
        per_slot = -(-len(pending) // nrc) if pending else 0
        cs = slice(ci * cw, (ci + 1) * cw)
        win = win_s.at[ci % 2]
        sh = sh_s.at[ci % 2]
        hist_s[:, cs] = win[tt:tt + D_HIST, :]
        for r in range(1, 8):
            sh[r - 1] = win[r:r + nsh, :]
        for ri, r0 in enumerate(range(0, tt, rc)):
            for stage in pending[ri * per_slot:(ri + 1) * per_slot]:
                stage()
            acc = jnp.broadcast_to(cb_ref[:, cs], (rc, cw))
            for k in range(D_CONV):
                j8, r = divmod(lead + k, 8)
                lo = 8 * j8 + r0
                taps = win[lo:lo + rc, :] if r == 0 else sh[r - 1, lo:lo + rc, :]
                acc = acc + cw_ref[k:k + 1, cs] * taps
            conv_s[r0:r0 + rc, cs] = acc

    nbuf_ref[0] = hist_s[lead:D_HIST, :]

    rn = 16

    def norm_body(i, carry):
        r0 = pl.multiple_of(i * rn, rn)
        c = conv_s[pl.ds(r0, rn), :]
        mu = jnp.mean(c, axis=-1, keepdims=True)
        xc = c - mu
        var = jnp.mean(xc * xc, axis=-1, keepdims=True)
        y = _silu(xc * lax.rsqrt(var + EPS) * lg_ref[...] + lb_ref[...])
        h_ref[0, pl.ds(r0, rn), :] = (y * _silu(gate_s[pl.ds(r0, rn), :])).astype(BF16)
        return carry

    lax.fori_loop(0, tt // rn, norm_body, 0, unroll=min(4, tt // rn))


def _conf_layer(x, g_pre, w_in, conv_w, conv_b, ln_g, ln_b, buf):
    b, t, d = x.shape
    tt = min(t, 256)
    bs = buf.shape[0]
    e = E_WIDTH
    row = lambda a: a.reshape(1, e)
    full = lambda shape: pl.BlockSpec(shape, lambda bi, ti: (0,) * len(shape))
    kern = functools.partial(_conf_layer_kernel, tt=tt)
    return pl.pallas_call(
        kern,
        grid=(b, t // tt),
        in_specs=[pl.BlockSpec((1, tt, d), lambda bi, ti: (bi, ti, 0)),
                  full((1, d)), _resident((d, 3 * e)),
                  full((D_CONV, e)), full((1, e)), full((1, e)), full((1, e)),
                  pl.BlockSpec((1, D_CONV - 1, e), lambda bi, ti: (bi if bs > 1 else 0, 0, 0))],
        out_specs=[pl.BlockSpec((1, tt, e), lambda bi, ti: (bi, ti, 0)),
                   pl.BlockSpec((1, D_CONV - 1, e), lambda bi, ti: (bi, 0, 0))],
        out_shape=[jax.ShapeDtypeStruct((b, t, e), BF16),
                   jax.ShapeDtypeStruct((b, D_CONV - 1, e), F32)],
        scratch_shapes=[pltpu.VMEM((D_HIST, e), F32), pltpu.VMEM((2, D_HIST + tt, 256), F32),
                        pltpu.VMEM((2, 7, tt + D_HIST - 8, 256), F32), pltpu.VMEM((tt, e), F32),
                        pltpu.VMEM((tt, e), F32)],
        compiler_params=_params(("parallel", "arbitrary")),
        name="conf_layer",
    )(x, g_pre.reshape(1, d), w_in, conv_w, row(conv_b), row(ln_g), row(ln_b), buf)


def _rope_tables(pos0, t):
    inv = 1.0 / (ROPE_BASE ** (jnp.arange(0, C_ROPE, 2, dtype=F32) / C_ROPE))
    ang = (pos0 + jnp.arange(t, dtype=jnp.int32)).astype(F32)[:, None] * inv[None, :]
    cos, sin = jnp.cos(ang), jnp.sin(ang)
    return jnp.tile(cos, (1, 4)), jnp.tile(jnp.concatenate([-sin, sin], axis=-1), (1, 2))


def _pad_keys(k):
    tk = k.shape[1]
    tkp = -(-tk // KEY_BLOCK) * KEY_BLOCK
    return jnp.pad(k.astype(BF16), ((0, 0), (0, tkp - tk), (0, 0)))


def _trunk(x, w, st, *, pos0, reset_first, chunked):
    b, t, d = x.shape
    m = b * t
    new = {}

    def layer(x, idx, mix):
        h = mix(x, w["norm_pre"][idx], w["w_in"][idx])
        return _out_res(h.reshape(m, E_WIDTH), w["w_out"][idx], w["norm_post"][idx], x.reshape(m, d)).reshape(b, t, d)

    def mix_a(x, g_pre, w_in):
        h, new["hgrn"] = _hgrn_layer(x, g_pre, w_in, w["a_lb_logits"], 0, w["a_norm_g"], st["hgrn"])
        return h

    def mix_b(x, g_pre, w_in):
        h, hl, new["rg_conv"] = _rglru_layer(x, g_pre, w_in, w["b_conv_w"], w["b_conv_b"], w["b_wa"], w["b_ba"],
                                             w["b_wx"], w["b_bx"], w["b_lambda"], st["rg_h"], st["rg_conv"],
                                             reset_first)
        new["rg_h"] = hl.reshape(b, E_WIDTH)
        return h

    def mix_c(x, g_pre, w_in):
        cos, sin = _rope_tables(pos0, t)
        gate, qa, qp, ckv, kpe = _mla_q(x, g_pre, w_in, cos, sin, w["c_q_norm"], w["c_kv_norm"], w["c_w_nope"],
                                        w["c_w_pe"], w["c_w_pe_sw"], w["c_w_uk_h"])
        new["mla_c"], new["mla_pe"] = ckv, kpe
        kc, kp = ckv, kpe
        n_prefix = 0
        if st["mla_c"] is not None:
            pc, pp = st["mla_c"], st["mla_pe"]
            n_prefix = pc.shape[1]
            pc = jnp.broadcast_to(pc, (b,) + pc.shape[1:])
            pp = jnp.broadcast_to(pp, (b,) + pp.shape[1:])
            kc = jnp.concatenate([pc.astype(BF16), ckv.astype(BF16)], axis=1)
            kp = jnp.concatenate([pp.astype(BF16), kpe.astype(BF16)], axis=1)
        return _mla_attn(qa, qp, _pad_keys(kc), _pad_keys(kp), gate, w["c_w_uv_h"], n_prefix, chunked)

    def mix_d(x, g_pre, w_in):
        h, new["conf"] = _conf_layer(x, g_pre, w_in, w["d_conv_w"], w["d_conv_b"], w["d_ln_g"], w["d_ln_b"],
                                     st["conf"])
        return h

    for idx, mix in enumerate((mix_a, mix_b, mix_c, mix_d)):
        x = layer(x, idx, mix)
    return x, new


def kernel(x_prompt, x_sample, state_hgrn, state_rglru_h, state_rglru_conv, cache_mla_latent, cache_mla_rope, state_conformer_conv, meta_tokens, norm_pre, norm_post, a_w_in, a_lb_logits, a_norm_g, a_w_out, b_w_in, b_conv_w, b_conv_b, b_wa, b_ba, b_wx, b_bx, b_lambda, b_w_out, c_w_in, c_q_norm, c_kv_norm, c_w_uq, c_w_uk, c_w_uv, c_w_out, d_w_in, d_conv_w, d_conv_b, d_ln_g, d_ln_b, d_w_out):
    assert norm_pre.shape[0] == 4, "one layer of each mixer type"
    bf = lambda a: a.astype(BF16)

    c_in = c_w_in[0]
    i1, i2, i3 = C_Q_LORA, C_Q_LORA + C_KV_LORA, C_Q_LORA + C_KV_LORA + C_ROPE
    half = C_ROPE // 2
    k_pe_cols = c_in[:, i2:i3]
    k_pe_sw = jnp.concatenate([k_pe_cols[:, half:], k_pe_cols[:, :half]], axis=1)
    c_in_perm = jnp.concatenate([c_in[:, i3:], c_in[:, :i3], k_pe_sw], axis=1)
    uq = c_w_uq[0].reshape(C_Q_LORA, C_HEADS, C_NOPE + C_ROPE)
    uq_pe = uq[:, :, C_NOPE:]
    uq_pe_sw = jnp.concatenate([uq_pe[..., half:], uq_pe[..., :half]], axis=-1)

    w = {
        "norm_pre": norm_pre, "norm_post": norm_post,
        "w_in": [bf(a_w_in[0]), bf(b_w_in[0]), bf(c_in_perm), bf(d_w_in[0])],
        "w_out": [bf(a_w_out[0]), bf(b_w_out[0]), bf(c_w_out[0]), bf(d_w_out[0])],
        "a_lb_logits": a_lb_logits, "a_norm_g": a_norm_g[0],
        "b_conv_w": b_conv_w[0], "b_conv_b": b_conv_b[0], "b_wa": bf(b_wa[0]), "b_ba": b_ba[0],
        "b_wx": bf(b_wx[0]), "b_bx": b_bx[0], "b_lambda": b_lambda[0],
        "c_q_norm": c_q_norm[0], "c_kv_norm": c_kv_norm[0],
        "c_w_nope": bf(uq[:, :, :C_NOPE].reshape(C_Q_LORA, C_HEADS * C_NOPE)),
        "c_w_pe": bf(uq_pe.reshape(C_Q_LORA, C_HEADS * C_ROPE)),
        "c_w_pe_sw": bf(uq_pe_sw.reshape(C_Q_LORA, C_HEADS * C_ROPE)),
        "c_w_uk_h": bf(jnp.transpose(c_w_uk[0].reshape(C_KV_LORA, C_HEADS, C_NOPE), (1, 2, 0))),
        "c_w_uv_h": bf(jnp.transpose(c_w_uv[0].reshape(C_KV_LORA, C_HEADS, C_V), (1, 0, 2))),
        "d_conv_w": d_conv_w[0], "d_conv_b": d_conv_b[0], "d_ln_g": d_ln_g[0], "d_ln_b": d_ln_b[0],
    }

    bp = x_prompt.shape[0]
    dt = x_prompt.dtype

    st_m = {"hgrn": jnp.zeros((1, A_HEADS, A_D, A_D), dt), "rg_h": jnp.zeros((1, E_WIDTH), dt),
            "rg_conv": jnp.zeros((1, B_CONV - 1, E_WIDTH), dt), "mla_c": None, "mla_pe": None,
            "conf": jnp.zeros((1, D_CONV - 1, E_WIDTH), dt)}
    _, new_m = _trunk(meta_tokens.astype(dt)[None], w, st_m, pos0=0, reset_first=True, chunked=False)

    st_p = {"hgrn": new_m["hgrn"], "rg_h": new_m["rg_h"], "rg_conv": new_m["rg_conv"],
            "mla_c": new_m["mla_c"], "mla_pe": new_m["mla_pe"], "conf": new_m["conf"]}
    yp, new_p = _trunk(x_prompt, w, st_p, pos0=N_META, reset_first=False, chunked=True)

    st_s = {"hgrn": state_hgrn[0], "rg_h": state_rglru_h[0], "rg_conv": state_rglru_conv[0],
            "mla_c": cache_mla_latent[0], "mla_pe": cache_mla_rope[0], "conf": state_conformer_conv[0]}
    ys, new_s = _trunk(x_sample, w, st_s, pos0=cache_mla_latent.shape[2], reset_first=False, chunked=False)

    def with_meta(meta_rows, rows):
        return jnp.concatenate([jnp.broadcast_to(meta_rows, (bp,) + meta_rows.shape[1:]), rows], axis=1)

    return (yp, ys,
            new_p["hgrn"][None], new_s["hgrn"][None],
            new_p["rg_h"][None], new_s["rg_h"][None],
            new_p["rg_conv"][None], new_s["rg_conv"][None],
            with_meta(new_m["mla_c"], new_p["mla_c"])[None], new_s["mla_c"][None],
            with_meta(new_m["mla_pe"], new_p["mla_pe"])[None], new_s["mla_pe"][None],
            new_p["conf"][None], new_s["conf"][None])
```

```python
import functools

import jax
import jax.numpy as jnp
from jax import lax
from jax.experimental import pallas as pl
from jax.experimental.pallas import tpu as pltpu

F32 = jnp.float32
BF16 = jnp.bfloat16

EPS = 1e-6
D_MODEL = 1024
E_WIDTH = 2048
N_META = 16
CHUNK = 64
A_HEADS = 16
A_D = 128
A_SUB = 16
A_CHUNK = 128
B_BLOCKS = 8
B_BS = 256
B_CONV = 4
B_C = 8.0
C_HEADS = 16
C_NOPE = 128
C_ROPE = 64
C_V = 128
C_Q_LORA = 512
C_KV_LORA = 256
C_SCALE = (C_NOPE + C_ROPE) ** -0.5
LOG2E = 1.4426950408889634
ROPE_BASE = 10000.0
KEY_BLOCK = 256
ATTN_TILES = 4
D_CONV = 31
D_HIST = 32

VMEM_LIMIT = 48 * 1024 * 1024


def _params(sem):
    return pltpu.CompilerParams(dimension_semantics=sem, vmem_limit_bytes=VMEM_LIMIT)


def _dot(a, b):
    return jnp.dot(a, b, preferred_element_type=F32)


def _dot_nt(a, b):
    return lax.dot_general(a, b, (((1,), (1,)), ((), ())), preferred_element_type=F32)


def _sigmoid(x):
    return jax.nn.sigmoid(x)


def _silu(x):
    return x * jax.nn.sigmoid(x)


def _rms(x, g):
    return x * lax.rsqrt(jnp.mean(x * x, axis=-1, keepdims=True) + EPS) * g


def _out_res_kernel(h_ref, w_ref, g_ref, x_ref, o_ref):
    y = _dot(h_ref[...], w_ref[...])
    o_ref[...] = x_ref[...] + _rms(y, g_ref[...])


def _out_res(h2d, w, g, x2d):
    m, e = h2d.shape
    d = w.shape[1]
    tm = min(m, 1024)
    return pl.pallas_call(
        _out_res_kernel,
        grid=(m // tm,),
        in_specs=[pl.BlockSpec((tm, e), lambda i: (i, 0)),
                  pl.BlockSpec((e, d), lambda i: (0, 0)),
                  pl.BlockSpec((1, d), lambda i: (0, 0)),
                  pl.BlockSpec((tm, d), lambda i: (i, 0))],
        out_specs=pl.BlockSpec((tm, d), lambda i: (i, 0)),
        out_shape=jax.ShapeDtypeStruct((m, d), F32),
        compiler_params=_params(("parallel",)),
        name="out_res",
    )(h2d, w, g.reshape(1, d), x2d)


def _cumsum_rows(x):
    n = x.shape[0]
    row = lax.broadcasted_iota(jnp.int32, x.shape, 0)
    s = 1
    while s < n:
        x = x + jnp.where(row >= s, pltpu.roll(x, s, axis=0), 0.0)
        s *= 2
    return x


def _piecewise_rows(cum, seg, pick):
    c, l = cum.shape
    pieces = []
    for i in range(c // seg):
        r = pick(i)
        if r is None:
            pieces.append(jnp.zeros((seg, l), F32))
        else:
            pieces.append(jnp.broadcast_to(cum[r:r + 1, :], (seg, l)))
    return pieces[0] if len(pieces) == 1 else jnp.concatenate(pieces, axis=0)


def _mla_q_kernel(x_ref, gpre_ref, win_ref, cos_ref, sin_ref, qn_ref, kvn_ref, wn_ref, wp_ref, wps_ref, wuk_ref,
                  g_ref, qa_ref, qp_ref, ckv_ref, kpe_ref):
    u = _rms(x_ref[0], gpre_ref[...]).astype(BF16)
    g_ref[0] = _dot(u, win_ref[:, :E_WIDTH])
    q_lat = _dot(u, win_ref[:, E_WIDTH:E_WIDTH + C_Q_LORA])
    kv = _dot(u, win_ref[:, E_WIDTH + C_Q_LORA:])
    ql = _rms(q_lat, qn_ref[...]).astype(BF16)
    q_nope = _dot(ql, wn_ref[...])
    q_pe = _dot(ql, wp_ref[...])
    q_sw = _dot(ql, wps_ref[...])
    cos = cos_ref[...]
    sin = sin_ref[...]
    for j in range(C_HEADS // 2):
        sl = slice(j * 128, (j + 1) * 128)
        r = (q_pe[:, sl] * cos + q_sw[:, sl] * sin).astype(BF16)
        qp_ref[0, 2 * j] = r[:, :C_ROPE]
        qp_ref[0, 2 * j + 1] = r[:, C_ROPE:]
    for h in range(C_HEADS):
        qh = q_nope[:, h * C_NOPE:(h + 1) * C_NOPE].astype(BF16)
        qa_ref[0, h] = _dot(qh, wuk_ref[h]).astype(BF16)
    ckv_ref[0] = _rms(kv[:, :C_KV_LORA], kvn_ref[...])
    k_pe = kv[:, C_KV_LORA:C_KV_LORA + C_ROPE]
    k_sw = kv[:, C_KV_LORA + C_ROPE:C_KV_LORA + 2 * C_ROPE]
    kpe_ref[0] = k_pe * cos[:, :C_ROPE] + k_sw * sin[:, :C_ROPE]


def _mla_q(x, g_pre, w_in, cos, sin, q_norm, kv_norm, w_nope, w_pe, w_pe_sw, w_uk_h):
    b, t, d = x.shape
    tm = min(t, 256)
    full = lambda shape: pl.BlockSpec(shape, lambda bi, ti: (0,) * len(shape))
    return pl.pallas_call(
        _mla_q_kernel,
        grid=(b, t // tm),
        in_specs=[pl.BlockSpec((1, tm, d), lambda bi, ti: (bi, ti, 0)),
                  full((1, d)), _resident(w_in.shape),
                  pl.BlockSpec((tm, 128), lambda bi, ti: (ti, 0)),
                  pl.BlockSpec((tm, 128), lambda bi, ti: (ti, 0)),
                  full((1, C_Q_LORA)), full((1, C_KV_LORA)),
                  full((C_Q_LORA, C_HEADS * C_NOPE)), full((C_Q_LORA, C_HEADS * C_ROPE)),
                  full((C_Q_LORA, C_HEADS * C_ROPE)), full((C_HEADS, C_NOPE, C_KV_LORA))],
        out_specs=[pl.BlockSpec((1, tm, E_WIDTH), lambda bi, ti: (bi, ti, 0)),
                   pl.BlockSpec((1, C_HEADS, tm, C_KV_LORA), lambda bi, ti: (bi, 0, ti, 0)),
                   pl.BlockSpec((1, C_HEADS, tm, C_ROPE), lambda bi, ti: (bi, 0, ti, 0)),
                   pl.BlockSpec((1, tm, C_KV_LORA), lambda bi, ti: (bi, ti, 0)),
                   pl.BlockSpec((1, tm, C_ROPE), lambda bi, ti: (bi, ti, 0))],
        out_shape=[jax.ShapeDtypeStruct((b, t, E_WIDTH), F32),
                   jax.ShapeDtypeStruct((b, C_HEADS, t, C_KV_LORA), BF16),
                   jax.ShapeDtypeStruct((b, C_HEADS, t, C_ROPE), BF16),
                   jax.ShapeDtypeStruct((b, t, C_KV_LORA), F32),
                   jax.ShapeDtypeStruct((b, t, C_ROPE), F32)],
        compiler_params=_params(("parallel", "arbitrary")),
        name="mla_q",
    )(x, g_pre.reshape(1, d), w_in, cos, sin, q_norm.reshape(1, -1), kv_norm.reshape(1, -1), w_nope, w_pe, w_pe_sw,
      w_uk_h)


def _mla_attn_kernel(qa_ref, qp_ref, kc_ref, kct_ref, kp_ref, g_ref, wuv_ref, h_ref, acc_s, s_s, p_s, *,
                     tq, nt, n_prefix, n_total, chunked):
    for k in range(nt):
        _mla_attn_tile(pl.program_id(1) * nt + k, slice(k * tq, (k + 1) * tq), qa_ref, qp_ref, kc_ref, kct_ref,
                       kp_ref, g_ref, wuv_ref, h_ref, acc_s, s_s, p_s, tq=tq, n_prefix=n_prefix, n_total=n_total,
                       chunked=chunked)


def _mla_attn_tile(ci, rows, qa_ref, qp_ref, kc_ref, kct_ref, kp_ref, g_ref, wuv_ref, h_ref, acc_s, s_s, p_s, *,
                   tq, n_prefix, n_total, chunked):
    r = C_HEADS * tq
    nv = (n_prefix + tq * (ci + 1)) if chunked else n_total
    nblk = (nv + KEY_BLOCK - 1) // KEY_BLOCK
    qa = qa_ref[0, :, rows, :].reshape(r, C_KV_LORA)
    qp = qp_ref[0, :, rows, :].reshape(r, C_ROPE)
    acc_s[...] = jnp.zeros((C_KV_LORA, r), F32)
    c2 = C_SCALE * LOG2E

    def scores(j):
        off = pl.multiple_of(j * KEY_BLOCK, KEY_BLOCK)
        return _dot_nt(kc_ref[0, pl.ds(off, KEY_BLOCK), :], qa) + _dot_nt(kp_ref[0, pl.ds(off, KEY_BLOCK), :], qp)

    def add_pv(j, alpha):
        off = pl.multiple_of(j * KEY_BLOCK, KEY_BLOCK)
        acc_s[...] = alpha * acc_s[...] + _dot(kct_ref[0, :, pl.ds(off, KEY_BLOCK)], p_s[...])

    s_s[...] = scores(0)

    def trip(j, carry, first=False, prefetch=True):
        m_old, l_old, alpha_prev = carry
        s = s_s[...]
        if not first:
            add_pv(j - 1, alpha_prev)
        if prefetch:
            s_s[...] = scores(jnp.minimum(j + 1, kc_ref.shape[1] // KEY_BLOCK - 1) if first else j + 1)
        key = j * KEY_BLOCK + lax.broadcasted_iota(jnp.int32, (KEY_BLOCK, 1), 0)
        s = jnp.where(key < nv, s, -jnp.inf)
        m_new = jnp.maximum(m_old, jnp.max(s, axis=0, keepdims=True))
        alpha = jnp.exp2((m_old - m_new) * c2)
        p = jnp.exp2((s - m_new) * c2)
        l_new = alpha * l_old + jnp.sum(p, axis=0, keepdims=True)
        p_s[...] = p.astype(BF16)
        return m_new, l_new, alpha

    init = (jnp.full((1, r), -jnp.inf, F32), jnp.zeros((1, r), F32), jnp.ones((1, r), F32))
    carry = trip(0, init, first=True)
    carry = lax.fori_loop(1, nblk - 1, trip, carry)
    carry = lax.cond(nblk >= 2, lambda cr: trip(nblk - 1, cr, prefetch=False), lambda cr: cr, carry)
    _, l_fin, alpha_fin = carry
    add_pv(nblk - 1, alpha_fin)
    o_lat = (acc_s[...] / l_fin).T.astype(BF16)
    for h in range(C_HEADS):
        oh = _dot(o_lat[h * tq:(h + 1) * tq], wuv_ref[h])
        sl = slice(h * C_V, (h + 1) * C_V)
        h_ref[0, rows, sl] = (oh * _silu(g_ref[0, rows, sl])).astype(BF16)


def _mla_attn(qa, qp, kc, kp, proj, w_uv_h, n_prefix, chunked):
    b, _, t, _ = qa.shape
    tq = min(t, CHUNK)
    nt = ATTN_TILES if t % (ATTN_TILES * tq) == 0 else 1
    tb = nt * tq
    n_total = n_prefix + t
    tk = kc.shape[1]
    r = C_HEADS * tq
    kct = jnp.swapaxes(kc, 1, 2)
    kern = functools.partial(_mla_attn_kernel, tq=tq, nt=nt, n_prefix=n_prefix, n_total=n_total, chunked=chunked)
    return pl.pallas_call(
        kern,
        grid=(b, t // tb),
        in_specs=[pl.BlockSpec((1, C_HEADS, tb, C_KV_LORA), lambda bi, ci: (bi, 0, ci, 0)),
                  pl.BlockSpec((1, C_HEADS, tb, C_ROPE), lambda bi, ci: (bi, 0, ci, 0)),
                  pl.BlockSpec((1, tk, C_KV_LORA), lambda bi, ci: (bi, 0, 0)),
                  pl.BlockSpec((1, C_KV_LORA, tk), lambda bi, ci: (bi, 0, 0)),
                  pl.BlockSpec((1, tk, C_ROPE), lambda bi, ci: (bi, 0, 0)),
                  pl.BlockSpec((1, tb, E_WIDTH), lambda bi, ci: (bi, ci, 0)),
                  pl.BlockSpec((C_HEADS, C_KV_LORA, C_V), lambda bi, ci: (0, 0, 0))],
        out_specs=pl.BlockSpec((1, tb, E_WIDTH), lambda bi, ci: (bi, ci, 0)),
        out_shape=jax.ShapeDtypeStruct((b, t, E_WIDTH), BF16),
        scratch_shapes=[pltpu.VMEM((C_KV_LORA, r), F32), pltpu.VMEM((KEY_BLOCK, r), F32),
                        pltpu.VMEM((KEY_BLOCK, r), BF16)],
        compiler_params=_params(("parallel", "arbitrary")),
        name="mla_attn",
    )(qa, qp, kc, kct, kp, proj, w_uv_h)


def _resident(shape):
    return pl.BlockSpec(shape, lambda *_: (0,) * len(shape), pipeline_mode=pl.Buffered(1))


def _hgrn_layer_kernel(x_ref, gpre_ref, wq_ref, wf_ref, wi_ref, wg_ref, lbl_ref, ng_ref, s0_ref, h_ref, sout_ref,
                       st_ref, pj_s, *, c, nch, hb, lb_row):
    t = pl.program_id(2)

    @pl.when(t == 0)
    def _():
        for h in range(hb):
            st_ref[h] = s0_ref[0, h].T

    lg = lbl_ref[...]
    e = jnp.exp(lg - jnp.max(lg, axis=0, keepdims=True))
    lb_all = jnp.sum(e[:lb_row + 1], axis=0, keepdims=True) / jnp.sum(e, axis=0, keepdims=True)

    rowi = lax.broadcasted_iota(jnp.int32, (c, c), 0)
    coli = lax.broadcasted_iota(jnp.int32, (c, c), 1)
    diag_mask = (rowi // A_SUB == coli // A_SUB) & (coli <= rowi)
    halves = []
    s = A_SUB
    while 2 * s <= c:
        halves.append(s)
        s *= 2
    row1 = lax.broadcasted_iota(jnp.int32, (c, A_D), 0)

    u = _rms(x_ref[0], gpre_ref[...]).astype(BF16)
    w_refs = (wq_ref, wf_ref, wi_ref, wg_ref)

    def project(hp, part):
        pj_s[hp, part] = _dot(u, w_refs[part][:, hp * 2 * A_D:(hp + 1) * 2 * A_D])

    for part in range(4):
        project(0, part)
    slots = [(ch, hh) for ch in range(nch) for hh in range(2)]
    per_slot = -(-4 // len(slots))
    for hp in range(hb // 2):
        for si, (ch, hh) in enumerate(slots):
            if hp + 1 < hb // 2:
                for part in range(si * per_slot, min(4, (si + 1) * per_slot)):
                    project(hp + 1, part)
            h = 2 * hp + hh
            sl = slice(h * A_D, (h + 1) * A_D)
            ln = slice(hh * A_D, (hh + 1) * A_D)
            rows = slice(ch * c, (ch + 1) * c)
            lb = lb_all[:, sl]
            q = _silu(pj_s[hp, 0, rows, ln])
            fg = lb + (1.0 - lb) * _sigmoid(pj_s[hp, 1, rows, ln])
            k = 1.0 - fg
            v = pj_s[hp, 2, rows, ln].astype(BF16)
            cum = _cumsum_rows(jnp.log(fg))

            start = _piecewise_rows(cum, A_SUB, lambda i: None if i == 0 else i * A_SUB - 1)
            rel = cum - start
            att = jnp.where(diag_mask,
                            _dot_nt((q * jnp.exp(rel)).astype(BF16), (k * jnp.exp(-rel)).astype(BF16)), 0.0)
            for s in halves:
                mid = _piecewise_rows(cum, 2 * s, lambda i: i * 2 * s + s - 1)
                second = (row1 % (2 * s)) >= s
                w = jnp.exp(-jnp.abs(cum - mid))
                ql = jnp.where(second, q * w, 0.0).astype(BF16)
                kl = jnp.where(second, 0.0, k * w).astype(BF16)
                lev = _dot_nt(ql, kl)
                if 2 * s < c:
                    lev = jnp.where(rowi // (2 * s) == coli // (2 * s), lev, 0.0)
                att = att + lev

            st = st_ref[h]
            o = _dot(att.astype(BF16), v) + _dot_nt((q * jnp.exp(cum)).astype(BF16), st.astype(BF16))
            last = cum[c - 1:c, :]
            kd = (k * jnp.exp(last - cum)).astype(BF16)
            st_ref[h] = st * jnp.exp(last) + _dot(v.T, kd)

            on = _rms(o, ng_ref[...])
            h_ref[0, rows, sl] = (on * _silu(pj_s[hp, 3, rows, ln])).astype(BF16)

    @pl.when(t == pl.num_programs(2) - 1)
    def _():
        for h in range(hb):
            sout_ref[0, h] = st_ref[h].T


def _hgrn_layer(x, g_pre, w_in, lb_logits, lb_row, norm_g, s0):
    b, t, d = x.shape
    c = min(t, A_CHUNK)
    nch = 4 if t % (4 * c) == 0 else 1
    tt = nch * c
    hb = 8
    nhb = A_HEADS // hb
    w = hb * A_D
    bs = s0.shape[0]
    wcol = lambda part: pl.BlockSpec((d, w), lambda hi, bi, ti: (0, part * nhb + hi))
    kern = functools.partial(_hgrn_layer_kernel, c=c, nch=nch, hb=hb, lb_row=lb_row)
    return pl.pallas_call(
        kern,
        grid=(nhb, b, t // tt),
        in_specs=[pl.BlockSpec((1, tt, d), lambda hi, bi, ti: (bi, ti, 0)),
                  pl.BlockSpec((1, d), lambda hi, bi, ti: (0, 0)),
                  wcol(0), wcol(1), wcol(2), wcol(3),
                  pl.BlockSpec((lb_logits.shape[0], w), lambda hi, bi, ti: (0, hi)),
                  pl.BlockSpec((1, A_D), lambda hi, bi, ti: (0, 0)),
                  pl.BlockSpec((1, hb, A_D, A_D), lambda hi, bi, ti: (bi if bs > 1 else 0, hi, 0, 0))],
        out_specs=[pl.BlockSpec((1, tt, w), lambda hi, bi, ti: (bi, ti, hi)),
                   pl.BlockSpec((1, hb, A_D, A_D), lambda hi, bi, ti: (bi, hi, 0, 0))],
        out_shape=[jax.ShapeDtypeStruct((b, t, E_WIDTH), BF16),
                   jax.ShapeDtypeStruct((b, A_HEADS, A_D, A_D), F32)],
        scratch_shapes=[pltpu.VMEM((hb, A_D, A_D), F32), pltpu.VMEM((hb // 2, 4, tt, 2 * A_D), F32)],
        compiler_params=_params(("parallel", "parallel", "arbitrary")),
        name="hgrn_layer",
    )(x, g_pre.reshape(1, d), w_in, w_in, w_in, w_in, lb_logits, norm_g.reshape(1, A_D), s0)


def _rglru_layer_kernel(x_ref, gpre_ref, win_ref, cw_ref, cb_ref, wa_ref, ba_ref, wx_ref, bx_ref, lam_ref, h0_ref,
                        buf_ref, h_ref, hl_ref, nbuf_ref, ext_s, a_s, b_s, gb_s, hc_s, *, tt, reset_first):
    t = pl.program_id(1)
    hist = 8
    nh = B_CONV - 1

    @pl.when(t == 0)
    def _():
        ext_s[hist - nh:hist, :] = buf_ref[0]
        hc_s[...] = h0_ref[0]

    u = _rms(x_ref[0], gpre_ref[...]).astype(BF16)
    lam = lam_ref[...]
    sp = jnp.maximum(-lam, 0.0) + jnp.log1p(jnp.exp(-jnp.abs(lam)))
    row = lax.broadcasted_iota(jnp.int32, (tt, B_BS), 0)
    def project_x(n):
        ext_s[hist:hist + tt, n * B_BS:(n + 1) * B_BS] = _dot(u, win_ref[:, n * B_BS:(n + 1) * B_BS])

    def project_g(n):
        gb_s[:, n * B_BS:(n + 1) * B_BS] = _dot(u, win_ref[:, E_WIDTH + n * B_BS:E_WIDTH + (n + 1) * B_BS])

    project_x(0)
    project_g(0)
    for n in range(B_BLOCKS):
        sl = slice(n * B_BS, (n + 1) * B_BS)
        xn = cb_ref[:, sl] + cw_ref[0:1, sl] * ext_s[hist - nh:hist - nh + tt, sl]
        for k in range(1, B_CONV):
            xn = xn + cw_ref[k:k + 1, sl] * ext_s[hist - nh + k:hist - nh + k + tt, sl]
        if n + 1 < B_BLOCKS:
            project_x(n + 1)
        xnb = xn.astype(BF16)
        r = _sigmoid(_dot(xnb, wa_ref[n]) + ba_ref[:, sl])
        if n + 1 < B_BLOCKS:
            project_g(n + 1)
        ig = _sigmoid(_dot(xnb, wx_ref[n]) + bx_ref[:, sl])
        log_a = -B_C * r * sp[:, sl]
        a = jnp.exp(log_a)
        mult = jnp.sqrt(jnp.tanh(-log_a) * (1.0 + a * a))
        if reset_first:
            mult = jnp.where((row == 0) & (t == 0), 1.0, mult)
        a_s[:, sl] = a
        b_s[:, sl] = mult * ig * xn

    tail = ext_s[hist + tt - nh:hist + tt, :]
    nbuf_ref[0] = tail
    ext_s[hist - nh:hist, :] = tail

    row8 = lax.broadcasted_iota(jnp.int32, (8, E_WIDTH), 0)

    def body(g, hc):
        r0 = pl.multiple_of(g * 8, 8)
        a = a_s[pl.ds(r0, 8), :]
        b = b_s[pl.ds(r0, 8), :]
        for s in (1, 2, 4):
            m = row8 >= s
            b = jnp.where(m, a * pltpu.roll(b, s, axis=0) + b, b)
            a = jnp.where(m, a * pltpu.roll(a, s, axis=0), a)
        hrows = a * hc + b
        h_ref[0, pl.ds(r0, 8), :] = (hrows * _silu(gb_s[pl.ds(r0, 8), :])).astype(BF16)
        return hrows[7:8, :]

    hc = lax.fori_loop(0, tt // 8, body, hc_s[...])
    hc_s[...] = hc
    hl_ref[0] = hc


def _rglru_layer(x, g_pre, w_in, conv_w, conv_b, wa, ba, wx, bx, lam, h0, buf, reset_first):
    b, t, d = x.shape
    tt = min(t, 256)
    bs = h0.shape[0]
    e = E_WIDTH
    row = lambda a: a.reshape(1, e)
    full = lambda shape: pl.BlockSpec(shape, lambda bi, ti: (0,) * len(shape))
    kern = functools.partial(_rglru_layer_kernel, tt=tt, reset_first=reset_first)
    return pl.pallas_call(
        kern,
        grid=(b, t // tt),
        in_specs=[pl.BlockSpec((1, tt, d), lambda bi, ti: (bi, ti, 0)),
                  full((1, d)), _resident((d, 2 * e)),
                  full((B_CONV, e)), full((1, e)),
                  full((B_BLOCKS, B_BS, B_BS)), full((1, e)),
                  full((B_BLOCKS, B_BS, B_BS)), full((1, e)), full((1, e)),
                  pl.BlockSpec((1, 1, e), lambda bi, ti: (bi if bs > 1 else 0, 0, 0)),
                  pl.BlockSpec((1, B_CONV - 1, e), lambda bi, ti: (bi if bs > 1 else 0, 0, 0))],
        out_specs=[pl.BlockSpec((1, tt, e), lambda bi, ti: (bi, ti, 0)),
                   pl.BlockSpec((1, 1, e), lambda bi, ti: (bi, 0, 0)),
                   pl.BlockSpec((1, B_CONV - 1, e), lambda bi, ti: (bi, 0, 0))],
        out_shape=[jax.ShapeDtypeStruct((b, t, e), BF16),
                   jax.ShapeDtypeStruct((b, 1, e), F32),
                   jax.ShapeDtypeStruct((b, B_CONV - 1, e), F32)],
        scratch_shapes=[pltpu.VMEM((8 + tt, e), F32), pltpu.VMEM((tt, e), F32), pltpu.VMEM((tt, e), F32),
                        pltpu.VMEM((tt, e), F32), pltpu.VMEM((1, e), F32)],
        compiler_params=_params(("parallel", "arbitrary")),
        name="rglru_layer",
    )(x, g_pre.reshape(1, d), w_in, conv_w, row(conv_b), wa, row(ba), wx, row(bx), row(lam), h0.reshape(bs, 1, e), buf)


def _conf_layer_kernel(x_ref, gpre_ref, win_ref, cw_ref, cb_ref, lg_ref, lb_ref, buf_ref, h_ref, nbuf_ref,
                       hist_s, win_s, sh_s, conv_s, gate_s, *, tt):
    t = pl.program_id(1)
    nhist = D_CONV - 1
    lead = D_HIST - nhist
    cw = 256
    nc = E_WIDTH // cw
    rc = min(tt, 64)
    nsh = tt + D_HIST - 8

    @pl.when(t == 0)
    def _():
        hist_s[0:lead, :] = jnp.zeros((lead, E_WIDTH), F32)
        hist_s[lead:D_HIST, :] = buf_ref[0]

    u = _rms(x_ref[0], gpre_ref[...]).astype(BF16)

    def project_stages(ci):
        cs = slice(ci * cw, (ci + 1) * cw)
        held = {}

        def glu_in():
            held["a"] = _dot(u, win_ref[:, cs])

        def glu_gate():
            b = _dot(u, win_ref[:, E_WIDTH + ci * cw:E_WIDTH + (ci + 1) * cw])
            win = win_s.at[ci % 2]
            win[0:D_HIST, :] = hist_s[:, cs]
            win[D_HIST:D_HIST + tt, :] = held["a"] * _sigmoid(b)

        def out_gate():
            gate_s[:, cs] = _dot(u, win_ref[:, 2 * E_WIDTH + ci * cw:2 * E_WIDTH + (ci + 1) * cw])

        return [glu_in, glu_gate, out_gate]

    for stage in project_stages(0):
        stage()
    nrc = tt // rc
    for ci in range(nc):
        pending = project_stages(ci + 1) if ci + 1 < nc else []
        per_slot = -(-len(pending) // nrc) if pending else 0
        cs = slice(ci * cw, (ci + 1) * cw)
        win = win_s.at[ci % 2]
        sh = sh_s.at[ci % 2]
        hist_s[:, cs] = win[tt:tt + D_HIST, :]
        for r in range(1, 8):
            sh[r - 1] = win[r:r + nsh, :]
        for ri, r0 in enumerate(range(0, tt, rc)):
            for stage in pending[ri * per_slot:(ri + 1) * per_slot]:
                stage()
            acc = jnp.broadcast_to(cb_ref[:, cs], (rc, cw))
            for k in range(D_CONV):
                j8, r = divmod(lead + k, 8)
                lo = 8 * j8 + r0
                taps = win[lo:lo + rc, :] if r == 0 else sh[r - 1, lo:lo + rc, :]
                acc = acc + cw_ref[k:k + 1, cs] * taps
            conv_s[r0:r0 + rc, cs] = acc

    nbuf_ref[0] = hist_s[lead:D_HIST, :]

    rn = 16

    def norm_body(i, carry):
        r0 = pl.multiple_of(i * rn, rn)
        c = conv_s[pl.ds(r0, rn), :]
        mu = jnp.mean(c, axis=-1, keepdims=True)
        xc = c - mu
        var = jnp.mean(xc * xc, axis=-1, keepdims=True)
        y = _silu(xc * lax.rsqrt(var + EPS) * lg_ref[...] + lb_ref[...])
        h_ref[0, pl.ds(r0, rn), :] = (y * _silu(gate_s[pl.ds(r0, rn), :])).astype(BF16)
        return carry

    lax.fori_loop(0, tt // rn, norm_body, 0, unroll=min(4, tt // rn))


def _conf_layer(x, g_pre, w_in, conv_w, conv_b, ln_g, ln_b, buf):
    b, t, d = x.shape
    tt = min(t, 256)
    bs = buf.shape[0]
    e = E_WIDTH
    row = lambda a: a.reshape(1, e)
    full = lambda shape: pl.BlockSpec(shape, lambda bi, ti: (0,) * len(shape))
    kern = functools.partial(_conf_layer_kernel, tt=tt)
    return pl.pallas_call(
        kern,
        grid=(b, t // tt),
        in_specs=[pl.BlockSpec((1, tt, d), lambda bi, ti: (bi, ti, 0)),
                  full((1, d)), _resident((d, 3 * e)),
                  full((D_CONV, e)), full((1, e)), full((1, e)), full((1, e)),
                  pl.BlockSpec((1, D_CONV - 1, e), lambda bi, ti: (bi if bs > 1 else 0, 0, 0))],
        out_specs=[pl.BlockSpec((1, tt, e), lambda bi, ti: (bi, ti, 0)),
                   pl.BlockSpec((1, D_CONV - 1, e), lambda bi, ti: (bi, 0, 0))],
        out_shape=[jax.ShapeDtypeStruct((b, t, e), BF16),
                   jax.ShapeDtypeStruct((b, D_CONV - 1, e), F32)],
        scratch_shapes=[pltpu.VMEM((D_HIST, e), F32), pltpu.VMEM((2, D_HIST + tt, 256), F32),
                        pltpu.VMEM((2, 7, tt + D_HIST - 8, 256), F32), pltpu.VMEM((tt, e), F32),
                        pltpu.VMEM((tt, e), F32)],
        compiler_params=_params(("parallel", "arbitrary")),
        name="conf_layer",
    )(x, g_pre.reshape(1, d), w_in, conv_w, row(conv_b), row(ln_g), row(ln_b), buf)


def _rope_tables(pos0, t):
    inv = 1.0 / (ROPE_BASE ** (jnp.arange(0, C_ROPE, 2, dtype=F32) / C_ROPE))
    ang = (pos0 + jnp.arange(t, dtype=jnp.int32)).astype(F32)[:, None] * inv[None, :]
    cos, sin = jnp.cos(ang), jnp.sin(ang)
    return jnp.tile(cos, (1, 4)), jnp.tile(jnp.concatenate([-sin, sin], axis=-1), (1, 2))


def _pad_keys(k):
    tk = k.shape[1]
    tkp = -(-tk // KEY_BLOCK) * KEY_BLOCK
    return jnp.pad(k.astype(BF16), ((0, 0), (0, tkp - tk), (0, 0)))


def _trunk(x, w, st, *, pos0, reset_first, chunked):
    b, t, d = x.shape
    m = b * t
    new = {}

    def layer(x, idx, mix):
        h = mix(x, w["norm_pre"][idx], w["w_in"][idx])
        return _out_res(h.reshape(m, E_WIDTH), w["w_out"][idx], w["norm_post"][idx], x.reshape(m, d)).reshape(b, t, d)

    def mix_a(x, g_pre, w_in):
        h, new["hgrn"] = _hgrn_layer(x, g_pre, w_in, w["a_lb_logits"], 0, w["a_norm_g"], st["hgrn"])
        return h

    def mix_b(x, g_pre, w_in):
        h, hl, new["rg_conv"] = _rglru_layer(x, g_pre, w_in, w["b_conv_w"], w["b_conv_b"], w["b_wa"], w["b_ba"],
                                             w["b_wx"], w["b_bx"], w["b_lambda"], st["rg_h"], st["rg_conv"],
                                             reset_first)
        new["rg_h"] = hl.reshape(b, E_WIDTH)
        return h

    def mix_c(x, g_pre, w_in):
        cos, sin = _rope_tables(pos0, t)
        gate, qa, qp, ckv, kpe = _mla_q(x, g_pre, w_in, cos, sin, w["c_q_norm"], w["c_kv_norm"], w["c_w_nope"],
                                        w["c_w_pe"], w["c_w_pe_sw"], w["c_w_uk_h"])
        new["mla_c"], new["mla_pe"] = ckv, kpe
        kc, kp = ckv, kpe
        n_prefix = 0
        if st["mla_c"] is not None:
            pc, pp = st["mla_c"], st["mla_pe"]
            n_prefix = pc.shape[1]
            pc = jnp.broadcast_to(pc, (b,) + pc.shape[1:])
            pp = jnp.broadcast_to(pp, (b,) + pp.shape[1:])
            kc = jnp.concatenate([pc.astype(BF16), ckv.astype(BF16)], axis=1)
            kp = jnp.concatenate([pp.astype(BF16), kpe.astype(BF16)], axis=1)
        return _mla_attn(qa, qp, _pad_keys(kc), _pad_keys(kp), gate, w["c_w_uv_h"], n_prefix, chunked)

    def mix_d(x, g_pre, w_in):
        h, new["conf"] = _conf_layer(x, g_pre, w_in, w["d_conv_w"], w["d_conv_b"], w["d_ln_g"], w["d_ln_b"],
                                     st["conf"])
        return h

    for idx, mix in enumerate((mix_a, mix_b, mix_c, mix_d)):
        x = layer(x, idx, mix)
    return x, new


def kernel(x_prompt, x_sample, state_hgrn, state_rglru_h, state_rglru_conv, cache_mla_latent, cache_mla_rope, state_conformer_conv, meta_tokens, norm_pre, norm_post, a_w_in, a_lb_logits, a_norm_g, a_w_out, b_w_in, b_conv_w, b_conv_b, b_wa, b_ba, b_wx, b_bx, b_lambda, b_w_out, c_w_in, c_q_norm, c_kv_norm, c_w_uq, c_w_uk, c_w_uv, c_w_out, d_w_in, d_conv_w, d_conv_b, d_ln_g, d_ln_b, d_w_out):
    assert norm_pre.shape[0] == 4, "one layer of each mixer type"
    bf = lambda a: a.astype(BF16)

    c_in = c_w_in[0]
    i1, i2, i3 = C_Q_LORA, C_Q_LORA + C_KV_LORA, C_Q_LORA + C_KV_LORA + C_ROPE
    half = C_ROPE // 2
    k_pe_cols = c_in[:, i2:i3]
    k_pe_sw = jnp.concatenate([k_pe_cols[:, half:], k_pe_cols[:, :half]], axis=1)
    c_in_perm = jnp.concatenate([c_in[:, i3:], c_in[:, :i3], k_pe_sw], axis=1)
    uq = c_w_uq[0].reshape(C_Q_LORA, C_HEADS, C_NOPE + C_ROPE)
    uq_pe = uq[:, :, C_NOPE:]
    uq_pe_sw = jnp.concatenate([uq_pe[..., half:], uq_pe[..., :half]], axis=-1)

    w = {
        "norm_pre": norm_pre, "norm_post": norm_post,
        "w_in": [bf(a_w_in[0]), bf(b_w_in[0]), bf(c_in_perm), bf(d_w_in[0])],
        "w_out": [bf(a_w_out[0]), bf(b_w_out[0]), bf(c_w_out[0]), bf(d_w_out[0])],
        "a_lb_logits": a_lb_logits, "a_norm_g": a_norm_g[0],
        "b_conv_w": b_conv_w[0], "b_conv_b": b_conv_b[0], "b_wa": bf(b_wa[0]), "b_ba": b_ba[0],
        "b_wx": bf(b_wx[0]), "b_bx": b_bx[0], "b_lambda": b_lambda[0],
        "c_q_norm": c_q_norm[0], "c_kv_norm": c_kv_norm[0],
        "c_w_nope": bf(uq[:, :, :C_NOPE].reshape(C_Q_LORA, C_HEADS * C_NOPE)),
        "c_w_pe": bf(uq_pe.reshape(C_Q_LORA, C_HEADS * C_ROPE)),
        "c_w_pe_sw": bf(uq_pe_sw.reshape(C_Q_LORA, C_HEADS * C_ROPE)),
        "c_w_uk_h": bf(jnp.transpose(c_w_uk[0].reshape(C_KV_LORA, C_HEADS, C_NOPE), (1, 2, 0))),
        "c_w_uv_h": bf(jnp.transpose(c_w_uv[0].reshape(C_KV_LORA, C_HEADS, C_V), (1, 0, 2))),
        "d_conv_w": d_conv_w[0], "d_conv_b": d_conv_b[0], "d_ln_g": d_ln_g[0], "d_ln_b": d_ln_b[0],
    }

    bp = x_prompt.shape[0]
    dt = x_prompt.dtype

    st_m = {"hgrn": jnp.zeros((1, A_HEADS, A_D, A_D), dt), "rg_h": jnp.zeros((1, E_WIDTH), dt),
            "rg_conv": jnp.zeros((1, B_CONV - 1, E_WIDTH), dt), "mla_c": None, "mla_pe": None,
            "conf": jnp.zeros((1, D_CONV - 1, E_WIDTH), dt)}
    _, new_m = _trunk(meta_tokens.astype(dt)[None], w, st_m, pos0=0, reset_first=True, chunked=False)

    st_p = {"hgrn": new_m["hgrn"], "rg_h": new_m["rg_h"], "rg_conv": new_m["rg_conv"],
            "mla_c": new_m["mla_c"], "mla_pe": new_m["mla_pe"], "conf": new_m["conf"]}
    yp, new_p = _trunk(x_prompt, w, st_p, pos0=N_META, reset_first=False, chunked=True)

    st_s = {"hgrn": state_hgrn[0], "rg_h": state_rglru_h[0], "rg_conv": state_rglru_conv[0],
            "mla_c": cache_mla_latent[0], "mla_pe": cache_mla_rope[0], "conf": state_conformer_conv[0]}
    ys, new_s = _trunk(x_sample, w, st_s, pos0=cache_mla_latent.shape[2], reset_first=False, chunked=False)

    def with_meta(meta_rows, rows):
        return jnp.concatenate([jnp.broadcast_to(meta_rows, (bp,) + meta_rows.shape[1:]), rows], axis=1)

    return (yp, ys,
            new_p["hgrn"][None], new_s["hgrn"][None],
            new_p["rg_h"][None], new_s["rg_h"][None],
            new_p["rg_conv"][None], new_s["rg_conv"][None],
            with_meta(new_m["mla_c"], new_p["mla_c"])[None], new_s["mla_c"][None],
            with_meta(new_m["mla_pe"], new_p["mla_pe"])[None], new_s["mla_pe"][None],
            new_p["conf"][None], new_s["conf"][None])
```

```python
import functools

import jax
import jax.numpy as jnp
from jax import lax
from jax.experimental import pallas as pl
from jax.experimental.pallas import tpu as pltpu

F32 = jnp.float32
BF16 = jnp.bfloat16

EPS = 1e-6
D_MODEL = 1024
E_WIDTH = 2048
N_META = 16
CHUNK = 64
A_HEADS = 16
A_D = 128
A_SUB = 16
A_CHUNK = 128
B_BLOCKS = 8
B_BS = 256
B_CONV = 4
B_C = 8.0
C_HEADS = 16
C_NOPE = 128
C_ROPE = 64
C_V = 128
C_Q_LORA = 512
C_KV_LORA = 256
C_SCALE = (C_NOPE + C_ROPE) ** -0.5
LOG2E = 1.4426950408889634
ROPE_BASE = 10000.0
KEY_BLOCK = 256
ATTN_TILES = 4
D_CONV = 31
D_HIST = 32

VMEM_LIMIT = 48 * 1024 * 1024


def _params(sem):
    return pltpu.CompilerParams(dimension_semantics=sem, vmem_limit_bytes=VMEM_LIMIT)


def _dot(a, b):
    return jnp.dot(a, b, preferred_element_type=F32)


def _dot_nt(a, b):
    return lax.dot_general(a, b, (((1,), (1,)), ((), ())), preferred_element_type=F32)


def _sigmoid(x):
    return jax.nn.sigmoid(x)


def _silu(x):
    return x * jax.nn.sigmoid(x)


def _rms(x, g):
    return x * lax.rsqrt(jnp.mean(x * x, axis=-1, keepdims=True) + EPS) * g


def _out_res_kernel(h_ref, w_ref, g_ref, x_ref, o_ref):
    y = _dot(h_ref[...], w_ref[...])
    o_ref[...] = x_ref[...] + _rms(y, g_ref[...])


def _out_res(h2d, w, g, x2d):
    m, e = h2d.shape
    d = w.shape[1]
    tm = min(m, 1024)
    return pl.pallas_call(
        _out_res_kernel,
        grid=(m // tm,),
        in_specs=[pl.BlockSpec((tm, e), lambda i: (i, 0)),
                  pl.BlockSpec((e, d), lambda i: (0, 0)),
                  pl.BlockSpec((1, d), lambda i: (0, 0)),
                  pl.BlockSpec((tm, d), lambda i: (i, 0))],
        out_specs=pl.BlockSpec((tm, d), lambda i: (i, 0)),
        out_shape=jax.ShapeDtypeStruct((m, d), F32),
        compiler_params=_params(("parallel",)),
        name="out_res",
    )(h2d, w, g.reshape(1, d), x2d)


def _cumsum_rows(x):
    n = x.shape[0]
    row = lax.broadcasted_iota(jnp.int32, x.shape, 0)
    s = 1
    while s < n:
        x = x + jnp.where(row >= s, pltpu.roll(x, s, axis=0), 0.0)
        s *= 2
    return x


def _piecewise_rows(cum, seg, pick):
    c, l = cum.shape
    pieces = []
    for i in range(c // seg):
        r = pick(i)
        if r is None:
            pieces.append(jnp.zeros((seg, l), F32))
        else:
            pieces.append(jnp.broadcast_to(cum[r:r + 1, :], (seg, l)))
    return pieces[0] if len(pieces) == 1 else jnp.concatenate(pieces, axis=0)


def _mla_q_kernel(x_ref, gpre_ref, win_ref, cos_ref, sin_ref, qn_ref, kvn_ref, wn_ref, wp_ref, wps_ref, wuk_ref,
                  g_ref, qa_ref, qp_ref, ckv_ref, kpe_ref):
    u = _rms(x_ref[0], gpre_ref[...]).astype(BF16)
    g_ref[0] = _dot(u, win_ref[:, :E_WIDTH])
    q_lat = _dot(u, win_ref[:, E_WIDTH:E_WIDTH + C_Q_LORA])
    kv = _dot(u, win_ref[:, E_WIDTH + C_Q_LORA:])
    ql = _rms(q_lat, qn_ref[...]).astype(BF16)
    q_nope = _dot(ql, wn_ref[...])
    q_pe = _dot(ql, wp_ref[...])
    q_sw = _dot(ql, wps_ref[...])
    cos = cos_ref[...]
    sin = sin_ref[...]
    for j in range(C_HEADS // 2):
        sl = slice(j * 128, (j + 1) * 128)
        r = (q_pe[:, sl] * cos + q_sw[:, sl] * sin).astype(BF16)
        qp_ref[0, 2 * j] = r[:, :C_ROPE]
        qp_ref[0, 2 * j + 1] = r[:, C_ROPE:]
    for h in range(C_HEADS):
        qh = q_nope[:, h * C_NOPE:(h + 1) * C_NOPE].astype(BF16)
        qa_ref[0, h] = _dot(qh, wuk_ref[h]).astype(BF16)
    ckv_ref[0] = _rms(kv[:, :C_KV_LORA], kvn_ref[...])
    k_pe = kv[:, C_KV_LORA:C_KV_LORA + C_ROPE]
    k_sw = kv[:, C_KV_LORA + C_ROPE:C_KV_LORA + 2 * C_ROPE]
    kpe_ref[0] = k_pe * cos[:, :C_ROPE] + k_sw * sin[:, :C_ROPE]


def _mla_q(x, g_pre, w_in, cos, sin, q_norm, kv_norm, w_nope, w_pe, w_pe_sw, w_uk_h):
    b, t, d = x.shape
    tm = min(t, 256)
    full = lambda shape: pl.BlockSpec(shape, lambda bi, ti: (0,) * len(shape))
    return pl.pallas_call(
        _mla_q_kernel,
        grid=(b, t // tm),
        in_specs=[pl.BlockSpec((1, tm, d), lambda bi, ti: (bi, ti, 0)),
                  full((1, d)), _resident(w_in.shape),
                  pl.BlockSpec((tm, 128), lambda bi, ti: (ti, 0)),
                  pl.BlockSpec((tm, 128), lambda bi, ti: (ti, 0)),
                  full((1, C_Q_LORA)), full((1, C_KV_LORA)),
                  full((C_Q_LORA, C_HEADS * C_NOPE)), full((C_Q_LORA, C_HEADS * C_ROPE)),
                  full((C_Q_LORA, C_HEADS * C_ROPE)), full((C_HEADS, C_NOPE, C_KV_LORA))],
        out_specs=[pl.BlockSpec((1, tm, E_WIDTH), lambda bi, ti: (bi, ti, 0)),
                   pl.BlockSpec((1, C_HEADS, tm, C_KV_LORA), lambda bi, ti: (bi, 0, ti, 0)),
                   pl.BlockSpec((1, C_HEADS, tm, C_ROPE), lambda bi, ti: (bi, 0, ti, 0)),
                   pl.BlockSpec((1, tm, C_KV_LORA), lambda bi, ti: (bi, ti, 0)),
                   pl.BlockSpec((1, tm, C_ROPE), lambda bi, ti: (bi, ti, 0))],
        out_shape=[jax.ShapeDtypeStruct((b, t, E_WIDTH), F32),
                   jax.ShapeDtypeStruct((b, C_HEADS, t, C_KV_LORA), BF16),
                   jax.ShapeDtypeStruct((b, C_HEADS, t, C_ROPE), BF16),
                   jax.ShapeDtypeStruct((b, t, C_KV_LORA), F32),
                   jax.ShapeDtypeStruct((b, t, C_ROPE), F32)],
        compiler_params=_params(("parallel", "arbitrary")),
        name="mla_q",
    )(x, g_pre.reshape(1, d), w_in, cos, sin, q_norm.reshape(1, -1), kv_norm.reshape(1, -1), w_nope, w_pe, w_pe_sw,
      w_uk_h)


def _mla_attn_kernel(qa_ref, qp_ref, kc_ref, kct_ref, kp_ref, g_ref, wuv_ref, h_ref, acc_s, s_s, p_s, *,
                     tq, nt, n_prefix, n_total, chunked):
    for k in range(nt):
        _mla_attn_tile(pl.program_id(1) * nt + k, slice(k * tq, (k + 1) * tq), qa_ref, qp_ref, kc_ref, kct_ref,
                       kp_ref, g_ref, wuv_ref, h_ref, acc_s, s_s, p_s, tq=tq, n_prefix=n_prefix, n_total=n_total,
                       chunked=chunked)


def _mla_attn_tile(ci, rows, qa_ref, qp_ref, kc_ref, kct_ref, kp_ref, g_ref, wuv_ref, h_ref, acc_s, s_s, p_s, *,
                   tq, n_prefix, n_total, chunked):
    r = C_HEADS * tq
    nv = (n_prefix + tq * (ci + 1)) if chunked else n_total
    kb = s_s.shape[0]
    nblk = (nv + kb - 1) // kb
    qa = qa_ref[0, :, rows, :].reshape(r, C_KV_LORA)
    qp = qp_ref[0, :, rows, :].reshape(r, C_ROPE)
    acc_s[...] = jnp.zeros((C_KV_LORA, r), F32)
    c2 = C_SCALE * LOG2E

    def scores(j):
        off = pl.multiple_of(j * kb, kb)
        return _dot_nt(kc_ref[0, pl.ds(off, kb), :], qa) + _dot_nt(kp_ref[0, pl.ds(off, kb), :], qp)

    def add_pv(j, alpha):
        off = pl.multiple_of(j * kb, kb)
        acc_s[...] = alpha * acc_s[...] + _dot(kct_ref[0, :, pl.ds(off, kb)], p_s[...])

    s_s[...] = scores(0)

    def trip(j, carry, first=False, prefetch=True):
        m_old, l_old, alpha_prev = carry
        s = s_s[...]
        if not first:
            add_pv(j - 1, alpha_prev)
        if prefetch:
            s_s[...] = scores(jnp.minimum(j + 1, kc_ref.shape[1] // kb - 1) if first else j + 1)
        key = j * kb + lax.broadcasted_iota(jnp.int32, (kb, 1), 0)
        s = jnp.where(key < nv, s, -jnp.inf)
        m_new = jnp.maximum(m_old, jnp.max(s, axis=0, keepdims=True))
        alpha = jnp.exp2((m_old - m_new) * c2)
        p = jnp.exp2((s - m_new) * c2)
        l_new = alpha * l_old + jnp.sum(p, axis=0, keepdims=True)
        p_s[...] = p.astype(BF16)
        return m_new, l_new, alpha

    init = (jnp.full((1, r), -jnp.inf, F32), jnp.zeros((1, r), F32), jnp.ones((1, r), F32))
    carry = trip(0, init, first=True)
    carry = lax.fori_loop(1, nblk - 1, trip, carry)
    carry = lax.cond(nblk >= 2, lambda cr: trip(nblk - 1, cr, prefetch=False), lambda cr: cr, carry)
    _, l_fin, alpha_fin = carry
    add_pv(nblk - 1, alpha_fin)
    o_lat = (acc_s[...] / l_fin).T.astype(BF16)
    for h in range(C_HEADS):
        oh = _dot(o_lat[h * tq:(h + 1) * tq], wuv_ref[h])
        sl = slice(h * C_V, (h + 1) * C_V)
        h_ref[0, rows, sl] = (oh * _silu(g_ref[0, rows, sl])).astype(BF16)


def _mla_attn(qa, qp, kc, kp, proj, w_uv_h, n_prefix, chunked):
    b, _, t, _ = qa.shape
    tq = min(t, CHUNK)
    nt = ATTN_TILES if t % (ATTN_TILES * tq) == 0 else 1
    tb = nt * tq
    n_total = n_prefix + t
    kb = KEY_BLOCK if chunked or n_total < 8 * KEY_BLOCK else 2 * KEY_BLOCK
    kc, kp = _pad_keys(kc, kb), _pad_keys(kp, kb)
    tk = kc.shape[1]
    r = C_HEADS * tq
    kct = jnp.swapaxes(kc, 1, 2)
    kern = functools.partial(_mla_attn_kernel, tq=tq, nt=nt, n_prefix=n_prefix, n_total=n_total, chunked=chunked)
    return pl.pallas_call(
        kern,
        grid=(b, t // tb),
        in_specs=[pl.BlockSpec((1, C_HEADS, tb, C_KV_LORA), lambda bi, ci: (bi, 0, ci, 0)),
                  pl.BlockSpec((1, C_HEADS, tb, C_ROPE), lambda bi, ci: (bi, 0, ci, 0)),
                  pl.BlockSpec((1, tk, C_KV_LORA), lambda bi, ci: (bi, 0, 0)),
                  pl.BlockSpec((1, C_KV_LORA, tk), lambda bi, ci: (bi, 0, 0)),
                  pl.BlockSpec((1, tk, C_ROPE), lambda bi, ci: (bi, 0, 0)),
                  pl.BlockSpec((1, tb, E_WIDTH), lambda bi, ci: (bi, ci, 0)),
                  pl.BlockSpec((C_HEADS, C_KV_LORA, C_V), lambda bi, ci: (0, 0, 0))],
        out_specs=pl.BlockSpec((1, tb, E_WIDTH), lambda bi, ci: (bi, ci, 0)),
        out_shape=jax.ShapeDtypeStruct((b, t, E_WIDTH), BF16),
        scratch_shapes=[pltpu.VMEM((C_KV_LORA, r), F32), pltpu.VMEM((kb, r), F32), pltpu.VMEM((kb, r), BF16)],
        compiler_params=_params(("parallel", "arbitrary")),
        name="mla_attn",
    )(qa, qp, kc, kct, kp, proj, w_uv_h)


def _resident(shape):
    return pl.BlockSpec(shape, lambda *_: (0,) * len(shape), pipeline_mode=pl.Buffered(1))


def _hgrn_layer_kernel(x_ref, gpre_ref, wq_ref, wf_ref, wi_ref, wg_ref, lbl_ref, ng_ref, s0_ref, h_ref, sout_ref,
                       st_ref, pj_s, *, c, nch, hb, lb_row):
    t = pl.program_id(2)

    @pl.when(t == 0)
    def _():
        for h in range(hb):
            st_ref[h] = s0_ref[0, h].T

    lg = lbl_ref[...]
    e = jnp.exp(lg - jnp.max(lg, axis=0, keepdims=True))
    lb_all = jnp.sum(e[:lb_row + 1], axis=0, keepdims=True) / jnp.sum(e, axis=0, keepdims=True)

    rowi = lax.broadcasted_iota(jnp.int32, (c, c), 0)
    coli = lax.broadcasted_iota(jnp.int32, (c, c), 1)
    diag_mask = (rowi // A_SUB == coli // A_SUB) & (coli <= rowi)
    halves = []
    s = A_SUB
    while 2 * s <= c:
        halves.append(s)
        s *= 2
    row1 = lax.broadcasted_iota(jnp.int32, (c, A_D), 0)

    u = _rms(x_ref[0], gpre_ref[...]).astype(BF16)
    w_refs = (wq_ref, wf_ref, wi_ref, wg_ref)

    def project(hp, part):
        pj_s[hp, part] = _dot(u, w_refs[part][:, hp * 2 * A_D:(hp + 1) * 2 * A_D])

    for part in range(4):
        project(0, part)
    slots = [(ch, hh) for ch in range(nch) for hh in range(2)]
    per_slot = -(-4 // len(slots))
    for hp in range(hb // 2):
        for si, (ch, hh) in enumerate(slots):
            if hp + 1 < hb // 2:
                for part in range(si * per_slot, min(4, (si + 1) * per_slot)):
                    project(hp + 1, part)
            h = 2 * hp + hh
            sl = slice(h * A_D, (h + 1) * A_D)
            ln = slice(hh * A_D, (hh + 1) * A_D)
            rows = slice(ch * c, (ch + 1) * c)
            lb = lb_all[:, sl]
            q = _silu(pj_s[hp, 0, rows, ln])
            fg = lb + (1.0 - lb) * _sigmoid(pj_s[hp, 1, rows, ln])
            k = 1.0 - fg
            v = pj_s[hp, 2, rows, ln].astype(BF16)
            cum = _cumsum_rows(jnp.log(fg))

            start = _piecewise_rows(cum, A_SUB, lambda i: None if i == 0 else i * A_SUB - 1)
            rel = cum - start
            att = jnp.where(diag_mask,
                            _dot_nt((q * jnp.exp(rel)).astype(BF16), (k * jnp.exp(-rel)).astype(BF16)), 0.0)
            for s in halves:
                mid = _piecewise_rows(cum, 2 * s, lambda i: i * 2 * s + s - 1)
                second = (row1 % (2 * s)) >= s
                w = jnp.exp(-jnp.abs(cum - mid))
                ql = jnp.where(second, q * w, 0.0).astype(BF16)
                kl = jnp.where(second, 0.0, k * w).astype(BF16)
                lev = _dot_nt(ql, kl)
                if 2 * s < c:
                    lev = jnp.where(rowi // (2 * s) == coli // (2 * s), lev, 0.0)
                att = att + lev

            st = st_ref[h]
            o = _dot(att.astype(BF16), v) + _dot_nt((q * jnp.exp(cum)).astype(BF16), st.astype(BF16))
            last = cum[c - 1:c, :]
            kd = (k * jnp.exp(last - cum)).astype(BF16)
            st_ref[h] = st * jnp.exp(last) + _dot(v.T, kd)

            on = _rms(o, ng_ref[...])
            h_ref[0, rows, sl] = (on * _silu(pj_s[hp, 3, rows, ln])).astype(BF16)

    @pl.when(t == pl.num_programs(2) - 1)
    def _():
        for h in range(hb):
            sout_ref[0, h] = st_ref[h].T


def _hgrn_layer(x, g_pre, w_in, lb_logits, lb_row, norm_g, s0):
    b, t, d = x.shape
    c = min(t, A_CHUNK)
    nch = 4 if t % (4 * c) == 0 else 1
    tt = nch * c
    hb = 8
    nhb = A_HEADS // hb
    w = hb * A_D
    bs = s0.shape[0]
    wcol = lambda part: pl.BlockSpec((d, w), lambda hi, bi, ti: (0, part * nhb + hi))
    kern = functools.partial(_hgrn_layer_kernel, c=c, nch=nch, hb=hb, lb_row=lb_row)
    return pl.pallas_call(
        kern,
        grid=(nhb, b, t // tt),
        in_specs=[pl.BlockSpec((1, tt, d), lambda hi, bi, ti: (bi, ti, 0)),
                  pl.BlockSpec((1, d), lambda hi, bi, ti: (0, 0)),
                  wcol(0), wcol(1), wcol(2), wcol(3),
                  pl.BlockSpec((lb_logits.shape[0], w), lambda hi, bi, ti: (0, hi)),
                  pl.BlockSpec((1, A_D), lambda hi, bi, ti: (0, 0)),
                  pl.BlockSpec((1, hb, A_D, A_D), lambda hi, bi, ti: (bi if bs > 1 else 0, hi, 0, 0))],
        out_specs=[pl.BlockSpec((1, tt, w), lambda hi, bi, ti: (bi, ti, hi)),
                   pl.BlockSpec((1, hb, A_D, A_D), lambda hi, bi, ti: (bi, hi, 0, 0))],
        out_shape=[jax.ShapeDtypeStruct((b, t, E_WIDTH), BF16),
                   jax.ShapeDtypeStruct((b, A_HEADS, A_D, A_D), F32)],
        scratch_shapes=[pltpu.VMEM((hb, A_D, A_D), F32), pltpu.VMEM((hb // 2, 4, tt, 2 * A_D), F32)],
        compiler_params=_params(("parallel", "parallel", "arbitrary")),
        name="hgrn_layer",
    )(x, g_pre.reshape(1, d), w_in, w_in, w_in, w_in, lb_logits, norm_g.reshape(1, A_D), s0)


def _rglru_layer_kernel(x_ref, gpre_ref, win_ref, cw_ref, cb_ref, wa_ref, ba_ref, wx_ref, bx_ref, lam_ref, h0_ref,
                        buf_ref, h_ref, hl_ref, nbuf_ref, ext_s, a_s, b_s, gb_s, hc_s, *, tt, reset_first):
    t = pl.program_id(1)
    hist = 8
    nh = B_CONV - 1

    @pl.when(t == 0)
    def _():
        ext_s[hist - nh:hist, :] = buf_ref[0]
        hc_s[...] = h0_ref[0]

    u = _rms(x_ref[0], gpre_ref[...]).astype(BF16)
    lam = lam_ref[...]
    sp = jnp.maximum(-lam, 0.0) + jnp.log1p(jnp.exp(-jnp.abs(lam)))
    row = lax.broadcasted_iota(jnp.int32, (tt, B_BS), 0)
    def project_x(n):
        ext_s[hist:hist + tt, n * B_BS:(n + 1) * B_BS] = _dot(u, win_ref[:, n * B_BS:(n + 1) * B_BS])

    def project_g(n):
        gb_s[:, n * B_BS:(n + 1) * B_BS] = _dot(u, win_ref[:, E_WIDTH + n * B_BS:E_WIDTH + (n + 1) * B_BS])

    project_x(0)
    project_g(0)
    for n in range(B_BLOCKS):
        sl = slice(n * B_BS, (n + 1) * B_BS)
        xn = cb_ref[:, sl] + cw_ref[0:1, sl] * ext_s[hist - nh:hist - nh + tt, sl]
        for k in range(1, B_CONV):
            xn = xn + cw_ref[k:k + 1, sl] * ext_s[hist - nh + k:hist - nh + k + tt, sl]
        if n + 1 < B_BLOCKS:
            project_x(n + 1)
        xnb = xn.astype(BF16)
        r = _sigmoid(_dot(xnb, wa_ref[n]) + ba_ref[:, sl])
        if n + 1 < B_BLOCKS:
            project_g(n + 1)
        ig = _sigmoid(_dot(xnb, wx_ref[n]) + bx_ref[:, sl])
        log_a = -B_C * r * sp[:, sl]
        a = jnp.exp(log_a)
        mult = jnp.sqrt(jnp.tanh(-log_a) * (1.0 + a * a))
        if reset_first:
            mult = jnp.where((row == 0) & (t == 0), 1.0, mult)
        a_s[:, sl] = a
        b_s[:, sl] = mult * ig * xn

    tail = ext_s[hist + tt - nh:hist + tt, :]
    nbuf_ref[0] = tail
    ext_s[hist - nh:hist, :] = tail

    row8 = lax.broadcasted_iota(jnp.int32, (8, E_WIDTH), 0)

    def body(g, hc):
        r0 = pl.multiple_of(g * 8, 8)
        a = a_s[pl.ds(r0, 8), :]
        b = b_s[pl.ds(r0, 8), :]
        for s in (1, 2, 4):
            m = row8 >= s
            b = jnp.where(m, a * pltpu.roll(b, s, axis=0) + b, b)
            a = jnp.where(m, a * pltpu.roll(a, s, axis=0), a)
        hrows = a * hc + b
        h_ref[0, pl.ds(r0, 8), :] = (hrows * _silu(gb_s[pl.ds(r0, 8), :])).astype(BF16)
        return hrows[7:8, :]

    hc = lax.fori_loop(0, tt // 8, body, hc_s[...], unroll=4 if tt % 32 == 0 else 1)
    hc_s[...] = hc
    hl_ref[0] = hc


def _rglru_layer(x, g_pre, w_in, conv_w, conv_b, wa, ba, wx, bx, lam, h0, buf, reset_first):
    b, t, d = x.shape
    tt = min(t, 256)
    bs = h0.shape[0]
    e = E_WIDTH
    row = lambda a: a.reshape(1, e)
    full = lambda shape: pl.BlockSpec(shape, lambda bi, ti: (0,) * len(shape))
    kern = functools.partial(_rglru_layer_kernel, tt=tt, reset_first=reset_first)
    return pl.pallas_call(
        kern,
        grid=(b, t // tt),
        in_specs=[pl.BlockSpec((1, tt, d), lambda bi, ti: (bi, ti, 0)),
                  full((1, d)), _resident((d, 2 * e)),
                  full((B_CONV, e)), full((1, e)),
                  full((B_BLOCKS, B_BS, B_BS)), full((1, e)),
                  full((B_BLOCKS, B_BS, B_BS)), full((1, e)), full((1, e)),
                  pl.BlockSpec((1, 1, e), lambda bi, ti: (bi if bs > 1 else 0, 0, 0)),
                  pl.BlockSpec((1, B_CONV - 1, e), lambda bi, ti: (bi if bs > 1 else 0, 0, 0))],
        out_specs=[pl.BlockSpec((1, tt, e), lambda bi, ti: (bi, ti, 0)),
                   pl.BlockSpec((1, 1, e), lambda bi, ti: (bi, 0, 0)),
                   pl.BlockSpec((1, B_CONV - 1, e), lambda bi, ti: (bi, 0, 0))],
        out_shape=[jax.ShapeDtypeStruct((b, t, e), BF16),
                   jax.ShapeDtypeStruct((b, 1, e), F32),
                   jax.ShapeDtypeStruct((b, B_CONV - 1, e), F32)],
        scratch_shapes=[pltpu.VMEM((8 + tt, e), F32), pltpu.VMEM((tt, e), F32), pltpu.VMEM((tt, e), F32),
                        pltpu.VMEM((tt, e), F32), pltpu.VMEM((1, e), F32)],
        compiler_params=_params(("parallel", "arbitrary")),
        name="rglru_layer",
    )(x, g_pre.reshape(1, d), w_in, conv_w, row(conv_b), wa, row(ba), wx, row(bx), row(lam), h0.reshape(bs, 1, e), buf)


def _conf_layer_kernel(x_ref, gpre_ref, win_ref, cw_ref, cb_ref, lg_ref, lb_ref, buf_ref, h_ref, nbuf_ref,
                       hist_s, win_s, sh_s, conv_s, gate_s, *, tt):
    t = pl.program_id(1)
    nhist = D_CONV - 1
    lead = D_HIST - nhist
    cw = 256
    nc = E_WIDTH // cw
    rc = min(tt, 64)
    nsh = tt + D_HIST - 8

    @pl.when(t == 0)
    def _():
        hist_s[0:lead, :] = jnp.zeros((lead, E_WIDTH), F32)
        hist_s[lead:D_HIST, :] = buf_ref[0]

    u = _rms(x_ref[0], gpre_ref[...]).astype(BF16)

    def project_stages(ci):
        cs = slice(ci * cw, (ci + 1) * cw)
        held = {}

        def glu_in():
            held["a"] = _dot(u, win_ref[:, cs])

        def glu_gate():
            b = _dot(u, win_ref[:, E_WIDTH + ci * cw:E_WIDTH + (ci + 1) * cw])
            win = win_s.at[ci % 2]
            win[0:D_HIST, :] = hist_s[:, cs]
            win[D_HIST:D_HIST + tt, :] = held["a"] * _sigmoid(b)

        def out_gate():
            gate_s[:, cs] = _dot(u, win_ref[:, 2 * E_WIDTH + ci * cw:2 * E_WIDTH + (ci + 1) * cw])

        return [glu_in, glu_gate, out_gate]

    for stage in project_stages(0):
        stage()
    nrc = tt // rc
    for ci in range(nc):
        pending = project_stages(ci + 1) if ci + 1 < nc else []
        per_slot = -(-len(pending) // nrc) if pending else 0
        cs = slice(ci * cw, (ci + 1) * cw)
        win = win_s.at[ci % 2]
        sh = sh_s.at[ci % 2]
        hist_s[:, cs] = win[tt:tt + D_HIST, :]
        for r in range(1, 8):
            sh[r - 1] = win[r:r + nsh, :]
        for ri, r0 in enumerate(range(0, tt, rc)):
            for stage in pending[ri * per_slot:(ri + 1) * per_slot]:
                stage()
            acc = jnp.broadcast_to(cb_ref[:, cs], (rc, cw))
            for k in range(D_CONV):
                j8, r = divmod(lead + k, 8)
                lo = 8 * j8 + r0
                taps = win[lo:lo + rc, :] if r == 0 else sh[r - 1, lo:lo + rc, :]
                acc = acc + cw_ref[k:k + 1, cs] * taps
            conv_s[r0:r0 + rc, cs] = acc

    nbuf_ref[0] = hist_s[lead:D_HIST, :]

    rn = 16

    def norm_body(i, carry):
        r0 = pl.multiple_of(i * rn, rn)
        c = conv_s[pl.ds(r0, rn), :]
        mu = jnp.mean(c, axis=-1, keepdims=True)
        xc = c - mu
        var = jnp.mean(xc * xc, axis=-1, keepdims=True)
        y = _silu(xc * lax.rsqrt(var + EPS) * lg_ref[...] + lb_ref[...])
        h_ref[0, pl.ds(r0, rn), :] = (y * _silu(gate_s[pl.ds(r0, rn), :])).astype(BF16)
        return carry

    lax.fori_loop(0, tt // rn, norm_body, 0, unroll=min(4, tt // rn))


def _conf_layer(x, g_pre, w_in, conv_w, conv_b, ln_g, ln_b, buf):
    b, t, d = x.shape
    tt = min(t, 256)
    bs = buf.shape[0]
    e = E_WIDTH
    row = lambda a: a.reshape(1, e)
    full = lambda shape: pl.BlockSpec(shape, lambda bi, ti: (0,) * len(shape))
    kern = functools.partial(_conf_layer_kernel, tt=tt)
    return pl.pallas_call(
        kern,
        grid=(b, t // tt),
        in_specs=[pl.BlockSpec((1, tt, d), lambda bi, ti: (bi, ti, 0)),
                  full((1, d)), _resident((d, 3 * e)),
                  full((D_CONV, e)), full((1, e)), full((1, e)), full((1, e)),
                  pl.BlockSpec((1, D_CONV - 1, e), lambda bi, ti: (bi if bs > 1 else 0, 0, 0))],
        out_specs=[pl.BlockSpec((1, tt, e), lambda bi, ti: (bi, ti, 0)),
                   pl.BlockSpec((1, D_CONV - 1, e), lambda bi, ti: (bi, 0, 0))],
        out_shape=[jax.ShapeDtypeStruct((b, t, e), BF16),
                   jax.ShapeDtypeStruct((b, D_CONV - 1, e), F32)],
        scratch_shapes=[pltpu.VMEM((D_HIST, e), F32), pltpu.VMEM((2, D_HIST + tt, 256), F32),
                        pltpu.VMEM((2, 7, tt + D_HIST - 8, 256), F32), pltpu.VMEM((tt, e), F32),
                        pltpu.VMEM((tt, e), F32)],
        compiler_params=_params(("parallel", "arbitrary")),
        name="conf_layer",
    )(x, g_pre.reshape(1, d), w_in, conv_w, row(conv_b), row(ln_g), row(ln_b), buf)


def _rope_tables(pos0, t):
    inv = 1.0 / (ROPE_BASE ** (jnp.arange(0, C_ROPE, 2, dtype=F32) / C_ROPE))
    ang = (pos0 + jnp.arange(t, dtype=jnp.int32)).astype(F32)[:, None] * inv[None, :]
    cos, sin = jnp.cos(ang), jnp.sin(ang)
    return jnp.tile(cos, (1, 4)), jnp.tile(jnp.concatenate([-sin, sin], axis=-1), (1, 2))


def _pad_keys(k, kb):
    tk = k.shape[1]
    tkp = -(-tk // kb) * kb
    return jnp.pad(k.astype(BF16), ((0, 0), (0, tkp - tk), (0, 0)))


def _trunk(x, w, st, *, pos0, reset_first, chunked):
    b, t, d = x.shape
    m = b * t
    new = {}

    def layer(x, idx, mix):
        h = mix(x, w["norm_pre"][idx], w["w_in"][idx])
        return _out_res(h.reshape(m, E_WIDTH), w["w_out"][idx], w["norm_post"][idx], x.reshape(m, d)).reshape(b, t, d)

    def mix_a(x, g_pre, w_in):
        h, new["hgrn"] = _hgrn_layer(x, g_pre, w_in, w["a_lb_logits"], 0, w["a_norm_g"], st["hgrn"])
        return h

    def mix_b(x, g_pre, w_in):
        h, hl, new["rg_conv"] = _rglru_layer(x, g_pre, w_in, w["b_conv_w"], w["b_conv_b"], w["b_wa"], w["b_ba"],
                                             w["b_wx"], w["b_bx"], w["b_lambda"], st["rg_h"], st["rg_conv"],
                                             reset_first)
        new["rg_h"] = hl.reshape(b, E_WIDTH)
        return h

    def mix_c(x, g_pre, w_in):
        cos, sin = _rope_tables(pos0, t)
        gate, qa, qp, ckv, kpe = _mla_q(x, g_pre, w_in, cos, sin, w["c_q_norm"], w["c_kv_norm"], w["c_w_nope"],
                                        w["c_w_pe"], w["c_w_pe_sw"], w["c_w_uk_h"])
        new["mla_c"], new["mla_pe"] = ckv, kpe
        kc, kp = ckv, kpe
        n_prefix = 0
        if st["mla_c"] is not None:
            pc, pp = st["mla_c"], st["mla_pe"]
            n_prefix = pc.shape[1]
            pc = jnp.broadcast_to(pc, (b,) + pc.shape[1:])
            pp = jnp.broadcast_to(pp, (b,) + pp.shape[1:])
            kc = jnp.concatenate([pc.astype(BF16), ckv.astype(BF16)], axis=1)
            kp = jnp.concatenate([pp.astype(BF16), kpe.astype(BF16)], axis=1)
        return _mla_attn(qa, qp, kc, kp, gate, w["c_w_uv_h"], n_prefix, chunked)

    def mix_d(x, g_pre, w_in):
        h, new["conf"] = _conf_layer(x, g_pre, w_in, w["d_conv_w"], w["d_conv_b"], w["d_ln_g"], w["d_ln_b"],
                                     st["conf"])
        return h

    for idx, mix in enumerate((mix_a, mix_b, mix_c, mix_d)):
        x = layer(x, idx, mix)
    return x, new


def kernel(x_prompt, x_sample, state_hgrn, state_rglru_h, state_rglru_conv, cache_mla_latent, cache_mla_rope, state_conformer_conv, meta_tokens, norm_pre, norm_post, a_w_in, a_lb_logits, a_norm_g, a_w_out, b_w_in, b_conv_w, b_conv_b, b_wa, b_ba, b_wx, b_bx, b_lambda, b_w_out, c_w_in, c_q_norm, c_kv_norm, c_w_uq, c_w_uk, c_w_uv, c_w_out, d_w_in, d_conv_w, d_conv_b, d_ln_g, d_ln_b, d_w_out):
    assert norm_pre.shape[0] == 4, "one layer of each mixer type"
    bf = lambda a: a.astype(BF16)

    c_in = c_w_in[0]
    i1, i2, i3 = C_Q_LORA, C_Q_LORA + C_KV_LORA, C_Q_LORA + C_KV_LORA + C_ROPE
    half = C_ROPE // 2
    k_pe_cols = c_in[:, i2:i3]
    k_pe_sw = jnp.concatenate([k_pe_cols[:, half:], k_pe_cols[:, :half]], axis=1)
    c_in_perm = jnp.concatenate([c_in[:, i3:], c_in[:, :i3], k_pe_sw], axis=1)
    uq = c_w_uq[0].reshape(C_Q_LORA, C_HEADS, C_NOPE + C_ROPE)
    uq_pe = uq[:, :, C_NOPE:]
    uq_pe_sw = jnp.concatenate([uq_pe[..., half:], uq_pe[..., :half]], axis=-1)

    w = {
        "norm_pre": norm_pre, "norm_post": norm_post,
        "w_in": [bf(a_w_in[0]), bf(b_w_in[0]), bf(c_in_perm), bf(d_w_in[0])],
        "w_out": [bf(a_w_out[0]), bf(b_w_out[0]), bf(c_w_out[0]), bf(d_w_out[0])],
        "a_lb_logits": a_lb_logits, "a_norm_g": a_norm_g[0],
        "b_conv_w": b_conv_w[0], "b_conv_b": b_conv_b[0], "b_wa": bf(b_wa[0]), "b_ba": b_ba[0],
        "b_wx": bf(b_wx[0]), "b_bx": b_bx[0], "b_lambda": b_lambda[0],
        "c_q_norm": c_q_norm[0], "c_kv_norm": c_kv_norm[0],
        "c_w_nope": bf(uq[:, :, :C_NOPE].reshape(C_Q_LORA, C_HEADS * C_NOPE)),
        "c_w_pe": bf(uq_pe.reshape(C_Q_LORA, C_HEADS * C_ROPE)),
        "c_w_pe_sw": bf(uq_pe_sw.reshape(C_Q_LORA, C_HEADS * C_ROPE)),
        "c_w_uk_h": bf(jnp.transpose(c_w_uk[0].reshape(C_KV_LORA, C_HEADS, C_NOPE), (1, 2, 0))),
        "c_w_uv_h": bf(jnp.transpose(c_w_uv[0].reshape(C_KV_LORA, C_HEADS, C_V), (1, 0, 2))),
        "d_conv_w": d_conv_w[0], "d_conv_b": d_conv_b[0], "d_ln_g": d_ln_g[0], "d_ln_b": d_ln_b[0],
    }

    bp = x_prompt.shape[0]
    dt = x_prompt.dtype

    st_m = {"hgrn": jnp.zeros((1, A_HEADS, A_D, A_D), dt), "rg_h": jnp.zeros((1, E_WIDTH), dt),
            "rg_conv": jnp.zeros((1, B_CONV - 1, E_WIDTH), dt), "mla_c": None, "mla_pe": None,
            "conf": jnp.zeros((1, D_CONV - 1, E_WIDTH), dt)}
    _, new_m = _trunk(meta_tokens.astype(dt)[None], w, st_m, pos0=0, reset_first=True, chunked=False)

    st_p = {"hgrn": new_m["hgrn"], "rg_h": new_m["rg_h"], "rg_conv": new_m["rg_conv"],
            "mla_c": new_m["mla_c"], "mla_pe": new_m["mla_pe"], "conf": new_m["conf"]}
    yp, new_p = _trunk(x_prompt, w, st_p, pos0=N_META, reset_first=False, chunked=True)

    st_s = {"hgrn": state_hgrn[0], "rg_h": state_rglru_h[0], "rg_conv": state_rglru_conv[0],
            "mla_c": cache_mla_latent[0], "mla_pe": cache_mla_rope[0], "conf": state_conformer_conv[0]}
    ys, new_s = _trunk(x_sample, w, st_s, pos0=cache_mla_latent.shape[2], reset_first=False, chunked=False)

    def with_meta(meta_rows, rows):
        return jnp.concatenate([jnp.broadcast_to(meta_rows, (bp,) + meta_rows.shape[1:]), rows], axis=1)

    return (yp, ys,
            new_p["hgrn"][None], new_s["hgrn"][None],
            new_p["rg_h"][None], new_s["rg_h"][None],
            new_p["rg_conv"][None], new_s["rg_conv"][None],
            with_meta(new_m["mla_c"], new_p["mla_c"])[None], new_s["mla_c"][None],
            with_meta(new_m["mla_pe"], new_p["mla_pe"])[None], new_s["mla_pe"][None],
            new_p["conf"][None], new_s["conf"][None])
```

```python
import functools

import jax
import jax.numpy as jnp
from jax import lax
from jax.experimental import pallas as pl
from jax.experimental.pallas import tpu as pltpu

F32 = jnp.float32
BF16 = jnp.bfloat16

EPS = 1e-6
D_MODEL = 1024
E_WIDTH = 2048
N_META = 16
CHUNK = 64
A_HEADS = 16
A_D = 128
A_SUB = 16
A_CHUNK = 128
B_BLOCKS = 8
B_BS = 256
B_CONV = 4
B_C = 8.0
C_HEADS = 16
C_NOPE = 128
C_ROPE = 64
C_V = 128
C_Q_LORA = 512
C_KV_LORA = 256
C_SCALE = (C_NOPE + C_ROPE) ** -0.5
LOG2E = 1.4426950408889634
ROPE_BASE = 10000.0
KEY_BLOCK = 256
ATTN_TILES = 4
D_CONV = 31
D_HIST = 32

VMEM_LIMIT = 48 * 1024 * 1024


def _params(sem):
    return pltpu.CompilerParams(dimension_semantics=sem, vmem_limit_bytes=VMEM_LIMIT)


def _dot(a, b):
    return jnp.dot(a, b, preferred_element_type=F32)


def _dot_nt(a, b):
    return lax.dot_general(a, b, (((1,), (1,)), ((), ())), preferred_element_type=F32)


def _sigmoid(x):
    return jax.nn.sigmoid(x)


def _silu(x):
    return x * jax.nn.sigmoid(x)


def _rms(x, g):
    return x * lax.rsqrt(jnp.mean(x * x, axis=-1, keepdims=True) + EPS) * g


def _out_res_kernel(h_ref, w_ref, g_ref, x_ref, o_ref):
    y = _dot(h_ref[...], w_ref[...])
    o_ref[...] = x_ref[...] + _rms(y, g_ref[...])


def _out_res(h2d, w, g, x2d):
    m, e = h2d.shape
    d = w.shape[1]
    tm = min(m, 1024)
    return pl.pallas_call(
        _out_res_kernel,
        grid=(m // tm,),
        in_specs=[pl.BlockSpec((tm, e), lambda i: (i, 0)),
                  pl.BlockSpec((e, d), lambda i: (0, 0)),
                  pl.BlockSpec((1, d), lambda i: (0, 0)),
                  pl.BlockSpec((tm, d), lambda i: (i, 0))],
        out_specs=pl.BlockSpec((tm, d), lambda i: (i, 0)),
        out_shape=jax.ShapeDtypeStruct((m, d), F32),
        compiler_params=_params(("parallel",)),
        name="out_res",
    )(h2d, w, g.reshape(1, d), x2d)


def _cumsum_rows(x):
    n = x.shape[0]
    row = lax.broadcasted_iota(jnp.int32, x.shape, 0)
    s = 1
    while s < n:
        x = x + jnp.where(row >= s, pltpu.roll(x, s, axis=0), 0.0)
        s *= 2
    return x


def _piecewise_rows(cum, seg, pick):
    c, l = cum.shape
    pieces = []
    for i in range(c // seg):
        r = pick(i)
        if r is None:
            pieces.append(jnp.zeros((seg, l), F32))
        else:
            pieces.append(jnp.broadcast_to(cum[r:r + 1, :], (seg, l)))
    return pieces[0] if len(pieces) == 1 else jnp.concatenate(pieces, axis=0)


def _mla_q_kernel(x_ref, gpre_ref, win_ref, cos_ref, sin_ref, qn_ref, kvn_ref, wn_ref, wp_ref, wps_ref, wuk_ref,
                  g_ref, qa_ref, qp_ref, ckv_ref, kpe_ref):
    u = _rms(x_ref[0], gpre_ref[...]).astype(BF16)
    g_ref[0] = _dot(u, win_ref[:, :E_WIDTH])
    q_lat = _dot(u, win_ref[:, E_WIDTH:E_WIDTH + C_Q_LORA])
    kv = _dot(u, win_ref[:, E_WIDTH + C_Q_LORA:])
    ql = _rms(q_lat, qn_ref[...]).astype(BF16)
    q_nope = _dot(ql, wn_ref[...])
    q_pe = _dot(ql, wp_ref[...])
    q_sw = _dot(ql, wps_ref[...])
    cos = cos_ref[...]
    sin = sin_ref[...]
    for j in range(C_HEADS // 2):
        sl = slice(j * 128, (j + 1) * 128)
        r = (q_pe[:, sl] * cos + q_sw[:, sl] * sin).astype(BF16)
        qp_ref[0, 2 * j] = r[:, :C_ROPE]
        qp_ref[0, 2 * j + 1] = r[:, C_ROPE:]
    for h in range(C_HEADS):
        qh = q_nope[:, h * C_NOPE:(h + 1) * C_NOPE].astype(BF16)
        qa_ref[0, h] = _dot(qh, wuk_ref[h]).astype(BF16)
    ckv_ref[0] = _rms(kv[:, :C_KV_LORA], kvn_ref[...])
    k_pe = kv[:, C_KV_LORA:C_KV_LORA + C_ROPE]
    k_sw = kv[:, C_KV_LORA + C_ROPE:C_KV_LORA + 2 * C_ROPE]
    kpe_ref[0] = k_pe * cos[:, :C_ROPE] + k_sw * sin[:, :C_ROPE]


def _mla_q(x, g_pre, w_in, cos, sin, q_norm, kv_norm, w_nope, w_pe, w_pe_sw, w_uk_h):
    b, t, d = x.shape
    tm = min(t, 256)
    full = lambda shape: pl.BlockSpec(shape, lambda bi, ti: (0,) * len(shape))
    return pl.pallas_call(
        _mla_q_kernel,
        grid=(b, t // tm),
        in_specs=[pl.BlockSpec((1, tm, d), lambda bi, ti: (bi, ti, 0)),
                  full((1, d)), _resident(w_in.shape),
                  pl.BlockSpec((tm, 128), lambda bi, ti: (ti, 0)),
                  pl.BlockSpec((tm, 128), lambda bi, ti: (ti, 0)),
                  full((1, C_Q_LORA)), full((1, C_KV_LORA)),
                  full((C_Q_LORA, C_HEADS * C_NOPE)), full((C_Q_LORA, C_HEADS * C_ROPE)),
                  full((C_Q_LORA, C_HEADS * C_ROPE)), full((C_HEADS, C_NOPE, C_KV_LORA))],
        out_specs=[pl.BlockSpec((1, tm, E_WIDTH), lambda bi, ti: (bi, ti, 0)),
                   pl.BlockSpec((1, C_HEADS, tm, C_KV_LORA), lambda bi, ti: (bi, 0, ti, 0)),
                   pl.BlockSpec((1, C_HEADS, tm, C_ROPE), lambda bi, ti: (bi, 0, ti, 0)),
                   pl.BlockSpec((1, tm, C_KV_LORA), lambda bi, ti: (bi, ti, 0)),
                   pl.BlockSpec((1, tm, C_ROPE), lambda bi, ti: (bi, ti, 0))],
        out_shape=[jax.ShapeDtypeStruct((b, t, E_WIDTH), F32),
                   jax.ShapeDtypeStruct((b, C_HEADS, t, C_KV_LORA), BF16),
                   jax.ShapeDtypeStruct((b, C_HEADS, t, C_ROPE), BF16),
                   jax.ShapeDtypeStruct((b, t, C_KV_LORA), F32),
                   jax.ShapeDtypeStruct((b, t, C_ROPE), F32)],
        compiler_params=_params(("parallel", "arbitrary")),
        name="mla_q",
    )(x, g_pre.reshape(1, d), w_in, cos, sin, q_norm.reshape(1, -1), kv_norm.reshape(1, -1), w_nope, w_pe, w_pe_sw,
      w_uk_h)


def _mla_attn_kernel(qa_ref, qp_ref, kc_ref, kct_ref, kp_ref, g_ref, wuv_ref, h_ref, acc_s, s_s, p_s, *,
                     tq, nt, n_prefix, n_total, chunked):
    for k in range(nt):
        _mla_attn_tile(pl.program_id(1) * nt + k, slice(k * tq, (k + 1) * tq), qa_ref, qp_ref, kc_ref, kct_ref,
                       kp_ref, g_ref, wuv_ref, h_ref, acc_s, s_s, p_s, tq=tq, n_prefix=n_prefix, n_total=n_total,
                       chunked=chunked)


def _mla_attn_tile(ci, rows, qa_ref, qp_ref, kc_ref, kct_ref, kp_ref, g_ref, wuv_ref, h_ref, acc_s, s_s, p_s, *,
                   tq, n_prefix, n_total, chunked):
    r = C_HEADS * tq
    nv = (n_prefix + tq * (ci + 1)) if chunked else n_total
    kb = s_s.shape[0]
    nblk = (nv + kb - 1) // kb
    qa = qa_ref[0, :, rows, :].reshape(r, C_KV_LORA)
    qp = qp_ref[0, :, rows, :].reshape(r, C_ROPE)
    acc_s[...] = jnp.zeros((C_KV_LORA, r), F32)
    c2 = C_SCALE * LOG2E

    def scores(j):
        off = pl.multiple_of(j * kb, kb)
        return _dot_nt(kc_ref[0, pl.ds(off, kb), :], qa) + _dot_nt(kp_ref[0, pl.ds(off, kb), :], qp)

    def add_pv(j, alpha):
        off = pl.multiple_of(j * kb, kb)
        acc_s[...] = alpha * acc_s[...] + _dot(kct_ref[0, :, pl.ds(off, kb)], p_s[...])

    s_s[...] = scores(0)

    def trip(j, carry, first=False, prefetch=True):
        m_old, l_old, alpha_prev = carry
        s = s_s[...]
        if not first:
            add_pv(j - 1, alpha_prev)
        if prefetch:
            s_s[...] = scores(jnp.minimum(j + 1, kc_ref.shape[1] // kb - 1) if first else j + 1)
        key = j * kb + lax.broadcasted_iota(jnp.int32, (kb, 1), 0)
        s = jnp.where(key < nv, s, -jnp.inf)
        m_new = jnp.maximum(m_old, jnp.max(s, axis=0, keepdims=True))
        alpha = jnp.exp2((m_old - m_new) * c2)
        p = jnp.exp2((s - m_new) * c2)
        l_new = alpha * l_old + jnp.sum(p, axis=0, keepdims=True)
        p_s[...] = p.astype(BF16)
        return m_new, l_new, alpha

    init = (jnp.full((1, r), -jnp.inf, F32), jnp.zeros((1, r), F32), jnp.ones((1, r), F32))
    carry = trip(0, init, first=True)
    carry = lax.fori_loop(1, nblk - 1, trip, carry)
    carry = lax.cond(nblk >= 2, lambda cr: trip(nblk - 1, cr, prefetch=False), lambda cr: cr, carry)
    _, l_fin, alpha_fin = carry
    add_pv(nblk - 1, alpha_fin)
    o_lat = (acc_s[...] / l_fin).T.astype(BF16)
    for h in range(C_HEADS):
        oh = _dot(o_lat[h * tq:(h + 1) * tq], wuv_ref[h])
        sl = slice(h * C_V, (h + 1) * C_V)
        h_ref[0, rows, sl] = (oh * _silu(g_ref[0, rows, sl])).astype(BF16)


def _mla_attn(qa, qp, kc, kp, proj, w_uv_h, n_prefix, chunked):
    b, _, t, _ = qa.shape
    tq = min(t, CHUNK)
    nt = ATTN_TILES if t % (ATTN_TILES * tq) == 0 else 1
    tb = nt * tq
    n_total = n_prefix + t
    kb = KEY_BLOCK
    kc, kp = _pad_keys(kc, kb), _pad_keys(kp, kb)
    tk = kc.shape[1]
    r = C_HEADS * tq
    kct = jnp.swapaxes(kc, 1, 2)
    kern = functools.partial(_mla_attn_kernel, tq=tq, nt=nt, n_prefix=n_prefix, n_total=n_total, chunked=chunked)
    return pl.pallas_call(
        kern,
        grid=(b, t // tb),
        in_specs=[pl.BlockSpec((1, C_HEADS, tb, C_KV_LORA), lambda bi, ci: (bi, 0, ci, 0)),
                  pl.BlockSpec((1, C_HEADS, tb, C_ROPE), lambda bi, ci: (bi, 0, ci, 0)),
                  pl.BlockSpec((1, tk, C_KV_LORA), lambda bi, ci: (bi, 0, 0)),
                  pl.BlockSpec((1, C_KV_LORA, tk), lambda bi, ci: (bi, 0, 0)),
                  pl.BlockSpec((1, tk, C_ROPE), lambda bi, ci: (bi, 0, 0)),
                  pl.BlockSpec((1, tb, E_WIDTH), lambda bi, ci: (bi, ci, 0)),
                  pl.BlockSpec((C_HEADS, C_KV_LORA, C_V), lambda bi, ci: (0, 0, 0))],
        out_specs=pl.BlockSpec((1, tb, E_WIDTH), lambda bi, ci: (bi, ci, 0)),
        out_shape=jax.ShapeDtypeStruct((b, t, E_WIDTH), BF16),
        scratch_shapes=[pltpu.VMEM((C_KV_LORA, r), F32), pltpu.VMEM((kb, r), F32), pltpu.VMEM((kb, r), BF16)],
        compiler_params=_params(("parallel", "arbitrary")),
        name="mla_attn",
    )(qa, qp, kc, kct, kp, proj, w_uv_h)


def _resident(shape):
    return pl.BlockSpec(shape, lambda *_: (0,) * len(shape), pipeline_mode=pl.Buffered(1))


def _hgrn_layer_kernel(x_ref, gpre_ref, wq_ref, wf_ref, wi_ref, wg_ref, lbl_ref, ng_ref, s0_ref, h_ref, sout_ref,
                       st_ref, pj_s, *, c, nch, hb, lb_row):
    t = pl.program_id(2)

    @pl.when(t == 0)
    def _():
        for h in range(hb):
            st_ref[h] = s0_ref[0, h].T

    lg = lbl_ref[...]
    e = jnp.exp(lg - jnp.max(lg, axis=0, keepdims=True))
    lb_all = jnp.sum(e[:lb_row + 1], axis=0, keepdims=True) / jnp.sum(e, axis=0, keepdims=True)

    rowi = lax.broadcasted_iota(jnp.int32, (c, c), 0)
    coli = lax.broadcasted_iota(jnp.int32, (c, c), 1)
    diag_mask = (rowi // A_SUB == coli // A_SUB) & (coli <= rowi)
    halves = []
    s = A_SUB
    while 2 * s <= c:
        halves.append(s)
        s *= 2
    row1 = lax.broadcasted_iota(jnp.int32, (c, A_D), 0)

    u = _rms(x_ref[0], gpre_ref[...]).astype(BF16)
    w_refs = (wq_ref, wf_ref, wi_ref, wg_ref)

    def project(hp, part):
        pj_s[hp, part] = _dot(u, w_refs[part][:, hp * 2 * A_D:(hp + 1) * 2 * A_D])

    for part in range(4):
        project(0, part)
    slots = [(ch, hh) for ch in range(nch) for hh in range(2)]
    per_slot = -(-4 // len(slots))
    for hp in range(hb // 2):
        for si, (ch, hh) in enumerate(slots):
            if hp + 1 < hb // 2:
                for part in range(si * per_slot, min(4, (si + 1) * per_slot)):
                    project(hp + 1, part)
            h = 2 * hp + hh
            sl = slice(h * A_D, (h + 1) * A_D)
            ln = slice(hh * A_D, (hh + 1) * A_D)
            rows = slice(ch * c, (ch + 1) * c)
            lb = lb_all[:, sl]
            q = _silu(pj_s[hp, 0, rows, ln])
            fg = lb + (1.0 - lb) * _sigmoid(pj_s[hp, 1, rows, ln])
            k = 1.0 - fg
            v = pj_s[hp, 2, rows, ln].astype(BF16)
            cum = _cumsum_rows(jnp.log(fg))

            start = _piecewise_rows(cum, A_SUB, lambda i: None if i == 0 else i * A_SUB - 1)
            rel = cum - start
            att = jnp.where(diag_mask,
                            _dot_nt((q * jnp.exp(rel)).astype(BF16), (k * jnp.exp(-rel)).astype(BF16)), 0.0)
            for s in halves:
                mid = _piecewise_rows(cum, 2 * s, lambda i: i * 2 * s + s - 1)
                second = (row1 % (2 * s)) >= s
                w = jnp.exp(-jnp.abs(cum - mid))
                ql = jnp.where(second, q * w, 0.0).astype(BF16)
                kl = jnp.where(second, 0.0, k * w).astype(BF16)
                lev = _dot_nt(ql, kl)
                if 2 * s < c:
                    lev = jnp.where(rowi // (2 * s) == coli // (2 * s), lev, 0.0)
                att = att + lev

            st = st_ref[h]
            o = _dot(att.astype(BF16), v) + _dot_nt((q * jnp.exp(cum)).astype(BF16), st.astype(BF16))
            last = cum[c - 1:c, :]
            kd = (k * jnp.exp(last - cum)).astype(BF16)
            st_ref[h] = st * jnp.exp(last) + _dot(v.T, kd)

            on = _rms(o, ng_ref[...])
            h_ref[0, rows, sl] = (on * _silu(pj_s[hp, 3, rows, ln])).astype(BF16)

    @pl.when(t == pl.num_programs(2) - 1)
    def _():
        for h in range(hb):
            sout_ref[0, h] = st_ref[h].T


def _hgrn_layer(x, g_pre, w_in, lb_logits, lb_row, norm_g, s0):
    b, t, d = x.shape
    c = min(t, A_CHUNK)
    nch = 4 if t % (4 * c) == 0 else 1
    tt = nch * c
    hb = 8
    nhb = A_HEADS // hb
    w = hb * A_D
    bs = s0.shape[0]
    wcol = lambda part: pl.BlockSpec((d, w), lambda hi, bi, ti: (0, part * nhb + hi))
    kern = functools.partial(_hgrn_layer_kernel, c=c, nch=nch, hb=hb, lb_row=lb_row)
    return pl.pallas_call(
        kern,
        grid=(nhb, b, t // tt),
        in_specs=[pl.BlockSpec((1, tt, d), lambda hi, bi, ti: (bi, ti, 0)),
                  pl.BlockSpec((1, d), lambda hi, bi, ti: (0, 0)),
                  wcol(0), wcol(1), wcol(2), wcol(3),
                  pl.BlockSpec((lb_logits.shape[0], w), lambda hi, bi, ti: (0, hi)),
                  pl.BlockSpec((1, A_D), lambda hi, bi, ti: (0, 0)),
                  pl.BlockSpec((1, hb, A_D, A_D), lambda hi, bi, ti: (bi if bs > 1 else 0, hi, 0, 0))],
        out_specs=[pl.BlockSpec((1, tt, w), lambda hi, bi, ti: (bi, ti, hi)),
                   pl.BlockSpec((1, hb, A_D, A_D), lambda hi, bi, ti: (bi, hi, 0, 0))],
        out_shape=[jax.ShapeDtypeStruct((b, t, E_WIDTH), BF16),
                   jax.ShapeDtypeStruct((b, A_HEADS, A_D, A_D), F32)],
        scratch_shapes=[pltpu.VMEM((hb, A_D, A_D), F32), pltpu.VMEM((hb // 2, 4, tt, 2 * A_D), F32)],
        compiler_params=_params(("parallel", "parallel", "arbitrary")),
        name="hgrn_layer",
    )(x, g_pre.reshape(1, d), w_in, w_in, w_in, w_in, lb_logits, norm_g.reshape(1, A_D), s0)


def _rglru_layer_kernel(x_ref, gpre_ref, win_ref, cw_ref, cb_ref, wa_ref, ba_ref, wx_ref, bx_ref, lam_ref, h0_ref,
                        buf_ref, h_ref, hl_ref, nbuf_ref, ext_s, a_s, b_s, gb_s, hc_s, *, tt, reset_first):
    t = pl.program_id(1)
    hist = 8
    nh = B_CONV - 1

    @pl.when(t == 0)
    def _():
        ext_s[hist - nh:hist, :] = buf_ref[0]
        hc_s[...] = h0_ref[0]

    u = _rms(x_ref[0], gpre_ref[...]).astype(BF16)
    lam = lam_ref[...]
    sp = jnp.maximum(-lam, 0.0) + jnp.log1p(jnp.exp(-jnp.abs(lam)))
    row = lax.broadcasted_iota(jnp.int32, (tt, B_BS), 0)
    def project_x(n):
        ext_s[hist:hist + tt, n * B_BS:(n + 1) * B_BS] = _dot(u, win_ref[:, n * B_BS:(n + 1) * B_BS])

    def project_g(n):
        gb_s[:, n * B_BS:(n + 1) * B_BS] = _dot(u, win_ref[:, E_WIDTH + n * B_BS:E_WIDTH + (n + 1) * B_BS])

    project_x(0)
    project_g(0)
    for n in range(B_BLOCKS):
        sl = slice(n * B_BS, (n + 1) * B_BS)
        xn = cb_ref[:, sl] + cw_ref[0:1, sl] * ext_s[hist - nh:hist - nh + tt, sl]
        for k in range(1, B_CONV):
            xn = xn + cw_ref[k:k + 1, sl] * ext_s[hist - nh + k:hist - nh + k + tt, sl]
        if n + 1 < B_BLOCKS:
            project_x(n + 1)
        xnb = xn.astype(BF16)
        r = _sigmoid(_dot(xnb, wa_ref[n]) + ba_ref[:, sl])
        if n + 1 < B_BLOCKS:
            project_g(n + 1)
        ig = _sigmoid(_dot(xnb, wx_ref[n]) + bx_ref[:, sl])
        log_a = -B_C * r * sp[:, sl]
        a = jnp.exp(log_a)
        mult = jnp.sqrt(jnp.tanh(-log_a) * (1.0 + a * a))
        if reset_first:
            mult = jnp.where((row == 0) & (t == 0), 1.0, mult)
        a_s[:, sl] = a
        b_s[:, sl] = mult * ig * xn

    tail = ext_s[hist + tt - nh:hist + tt, :]
    nbuf_ref[0] = tail
    ext_s[hist - nh:hist, :] = tail

    row8 = lax.broadcasted_iota(jnp.int32, (8, E_WIDTH), 0)

    def body(g, hc):
        r0 = pl.multiple_of(g * 8, 8)
        a = a_s[pl.ds(r0, 8), :]
        b = b_s[pl.ds(r0, 8), :]
        for s in (1, 2, 4):
            m = row8 >= s
            b = jnp.where(m, a * pltpu.roll(b, s, axis=0) + b, b)
            a = jnp.where(m, a * pltpu.roll(a, s, axis=0), a)
        hrows = a * hc + b
        h_ref[0, pl.ds(r0, 8), :] = (hrows * _silu(gb_s[pl.ds(r0, 8), :])).astype(BF16)
        return hrows[7:8, :]

    hc = lax.fori_loop(0, tt // 8, body, hc_s[...], unroll=4 if tt % 32 == 0 else 1)
    hc_s[...] = hc
    hl_ref[0] = hc


def _rglru_layer(x, g_pre, w_in, conv_w, conv_b, wa, ba, wx, bx, lam, h0, buf, reset_first):
    b, t, d = x.shape
    tt = min(t, 256)
    bs = h0.shape[0]
    e = E_WIDTH
    row = lambda a: a.reshape(1, e)
    full = lambda shape: pl.BlockSpec(shape, lambda bi, ti: (0,) * len(shape))
    kern = functools.partial(_rglru_layer_kernel, tt=tt, reset_first=reset_first)
    return pl.pallas_call(
        kern,
        grid=(b, t // tt),
        in_specs=[pl.BlockSpec((1, tt, d), lambda bi, ti: (bi, ti, 0)),
                  full((1, d)), _resident((d, 2 * e)),
                  full((B_CONV, e)), full((1, e)),
                  full((B_BLOCKS, B_BS, B_BS)), full((1, e)),
                  full((B_BLOCKS, B_BS, B_BS)), full((1, e)), full((1, e)),
                  pl.BlockSpec((1, 1, e), lambda bi, ti: (bi if bs > 1 else 0, 0, 0)),
                  pl.BlockSpec((1, B_CONV - 1, e), lambda bi, ti: (bi if bs > 1 else 0, 0, 0))],
        out_specs=[pl.BlockSpec((1, tt, e), lambda bi, ti: (bi, ti, 0)),
                   pl.BlockSpec((1, 1, e), lambda bi, ti: (bi, 0, 0)),
                   pl.BlockSpec((1, B_CONV - 1, e), lambda bi, ti: (bi, 0, 0))],
        out_shape=[jax.ShapeDtypeStruct((b, t, e), BF16),
                   jax.ShapeDtypeStruct((b, 1, e), F32),
                   jax.ShapeDtypeStruct((b, B_CONV - 1, e), F32)],
        scratch_shapes=[pltpu.VMEM((8 + tt, e), F32), pltpu.VMEM((tt, e), F32), pltpu.VMEM((tt, e), F32),
                        pltpu.VMEM((tt, e), F32), pltpu.VMEM((1, e), F32)],
        compiler_params=_params(("parallel", "arbitrary")),
        name="rglru_layer",
    )(x, g_pre.reshape(1, d), w_in, conv_w, row(conv_b), wa, row(ba), wx, row(bx), row(lam), h0.reshape(bs, 1, e), buf)


def _conf_layer_kernel(x_ref, gpre_ref, win_ref, cw_ref, cb_ref, lg_ref, lb_ref, buf_ref, h_ref, nbuf_ref,
                       hist_s, win_s, sh_s, conv_s, gate_s, *, tt):
    t = pl.program_id(1)
    nhist = D_CONV - 1
    lead = D_HIST - nhist
    cw = 256
    nc = E_WIDTH // cw
    rc = min(tt, 64)
    nsh = tt + D_HIST - 8

    @pl.when(t == 0)
    def _():
        hist_s[0:lead, :] = jnp.zeros((lead, E_WIDTH), F32)
        hist_s[lead:D_HIST, :] = buf_ref[0]

    u = _rms(x_ref[0], gpre_ref[...]).astype(BF16)

    def project_stages(ci):
        cs = slice(ci * cw, (ci + 1) * cw)
        held = {}

        def glu_in():
            held["a"] = _dot(u, win_ref[:, cs])

        def glu_gate():
            b = _dot(u, win_ref[:, E_WIDTH + ci * cw:E_WIDTH + (ci + 1) * cw])
            win = win_s.at[ci % 2]
            win[0:D_HIST, :] = hist_s[:, cs]
            win[D_HIST:D_HIST + tt, :] = held["a"] * _sigmoid(b)

        def out_gate():
            gate_s[:, cs] = _dot(u, win_ref[:, 2 * E_WIDTH + ci * cw:2 * E_WIDTH + (ci + 1) * cw])

        return [glu_in, glu_gate, out_gate]

    for stage in project_stages(0):
        stage()
    nrc = tt // rc
    for ci in range(nc):
        pending = project_stages(ci + 1) if ci + 1 < nc else []
        per_slot = -(-len(pending) // nrc) if pending else 0
        cs = slice(ci * cw, (ci + 1) * cw)
        win = win_s.at[ci % 2]
        sh = sh_s.at[ci % 2]
        hist_s[:, cs] = win[tt:tt + D_HIST, :]
        for r in range(1, 8):
            sh[r - 1] = win[r:r + nsh, :]
        for ri, r0 in enumerate(range(0, tt, rc)):
            for stage in pending[ri * per_slot:(ri + 1) * per_slot]:
                stage()
            acc = jnp.broadcast_to(cb_ref[:, cs], (rc, cw))
            for k in range(D_CONV):
                j8, r = divmod(lead + k, 8)
                lo = 8 * j8 + r0
                taps = win[lo:lo + rc, :] if r == 0 else sh[r - 1, lo:lo + rc, :]
                acc = acc + cw_ref[k:k + 1, cs] * taps
            conv_s[r0:r0 + rc, cs] = acc

    nbuf_ref[0] = hist_s[lead:D_HIST, :]

    rn = 16

    def norm_body(i, carry):
        r0 = pl.multiple_of(i * rn, rn)
        c = conv_s[pl.ds(r0, rn), :]
        mu = jnp.mean(c, axis=-1, keepdims=True)
        xc = c - mu
        var = jnp.mean(xc * xc, axis=-1, keepdims=True)
        y = _silu(xc * lax.rsqrt(var + EPS) * lg_ref[...] + lb_ref[...])
        h_ref[0, pl.ds(r0, rn), :] = (y * _silu(gate_s[pl.ds(r0, rn), :])).astype(BF16)
        return carry

    lax.fori_loop(0, tt // rn, norm_body, 0, unroll=min(4, tt // rn))


def _conf_layer(x, g_pre, w_in, conv_w, conv_b, ln_g, ln_b, buf):
    b, t, d = x.shape
    tt = min(t, 256)
    bs = buf.shape[0]
    e = E_WIDTH
    row = lambda a: a.reshape(1, e)
    full = lambda shape: pl.BlockSpec(shape, lambda bi, ti: (0,) * len(shape))
    kern = functools.partial(_conf_layer_kernel, tt=tt)
    return pl.pallas_call(
        kern,
        grid=(b, t // tt),
        in_specs=[pl.BlockSpec((1, tt, d), lambda bi, ti: (bi, ti, 0)),
                  full((1, d)), _resident((d, 3 * e)),
                  full((D_CONV, e)), full((1, e)), full((1, e)), full((1, e)),
                  pl.BlockSpec((1, D_CONV - 1, e), lambda bi, ti: (bi if bs > 1 else 0, 0, 0))],
        out_specs=[pl.BlockSpec((1, tt, e), lambda bi, ti: (bi, ti, 0)),
                   pl.BlockSpec((1, D_CONV - 1, e), lambda bi, ti: (bi, 0, 0))],
        out_shape=[jax.ShapeDtypeStruct((b, t, e), BF16),
                   jax.ShapeDtypeStruct((b, D_CONV - 1, e), F32)],
        scratch_shapes=[pltpu.VMEM((D_HIST, e), F32), pltpu.VMEM((2, D_HIST + tt, 256), F32),
                        pltpu.VMEM((2, 7, tt + D_HIST - 8, 256), F32), pltpu.VMEM((tt, e), F32),
                        pltpu.VMEM((tt, e), F32)],
        compiler_params=_params(("parallel", "arbitrary")),
        name="conf_layer",
    )(x, g_pre.reshape(1, d), w_in, conv_w, row(conv_b), row(ln_g), row(ln_b), buf)


def _rope_tables(pos0, t):
    inv = 1.0 / (ROPE_BASE ** (jnp.arange(0, C_ROPE, 2, dtype=F32) / C_ROPE))
    ang = (pos0 + jnp.arange(t, dtype=jnp.int32)).astype(F32)[:, None] * inv[None, :]
    cos, sin = jnp.cos(ang), jnp.sin(ang)
    return jnp.tile(cos, (1, 4)), jnp.tile(jnp.concatenate([-sin, sin], axis=-1), (1, 2))


def _pad_keys(k, kb):
    tk = k.shape[1]
    tkp = -(-tk // kb) * kb
    return jnp.pad(k.astype(BF16), ((0, 0), (0, tkp - tk), (0, 0)))


def _trunk(x, w, st, *, pos0, reset_first, chunked):
    b, t, d = x.shape
    m = b * t
    new = {}

    def layer(x, idx, mix):
        h = mix(x, w["norm_pre"][idx], w["w_in"][idx])
        return _out_res(h.reshape(m, E_WIDTH), w["w_out"][idx], w["norm_post"][idx], x.reshape(m, d)).reshape(b, t, d)

    def mix_a(x, g_pre, w_in):
        h, new["hgrn"] = _hgrn_layer(x, g_pre, w_in, w["a_lb_logits"], 0, w["a_norm_g"], st["hgrn"])
        return h

    def mix_b(x, g_pre, w_in):
        h, hl, new["rg_conv"] = _rglru_layer(x, g_pre, w_in, w["b_conv_w"], w["b_conv_b"], w["b_wa"], w["b_ba"],
                                             w["b_wx"], w["b_bx"], w["b_lambda"], st["rg_h"], st["rg_conv"],
                                             reset_first)
        new["rg_h"] = hl.reshape(b, E_WIDTH)
        return h

    def mix_c(x, g_pre, w_in):
        cos, sin = _rope_tables(pos0, t)
        gate, qa, qp, ckv, kpe = _mla_q(x, g_pre, w_in, cos, sin, w["c_q_norm"], w["c_kv_norm"], w["c_w_nope"],
                                        w["c_w_pe"], w["c_w_pe_sw"], w["c_w_uk_h"])
        new["mla_c"], new["mla_pe"] = ckv, kpe
        kc, kp = ckv, kpe
        n_prefix = 0
        if st["mla_c"] is not None:
            pc, pp = st["mla_c"], st["mla_pe"]
            n_prefix = pc.shape[1]
            pc = jnp.broadcast_to(pc, (b,) + pc.shape[1:])
            pp = jnp.broadcast_to(pp, (b,) + pp.shape[1:])
            kc = jnp.concatenate([pc.astype(BF16), ckv.astype(BF16)], axis=1)
            kp = jnp.concatenate([pp.astype(BF16), kpe.astype(BF16)], axis=1)
        return _mla_attn(qa, qp, kc, kp, gate, w["c_w_uv_h"], n_prefix, chunked)

    def mix_d(x, g_pre, w_in):
        h, new["conf"] = _conf_layer(x, g_pre, w_in, w["d_conv_w"], w["d_conv_b"], w["d_ln_g"], w["d_ln_b"],
                                     st["conf"])
        return h

    for idx, mix in enumerate((mix_a, mix_b, mix_c, mix_d)):
        x = layer(x, idx, mix)
    return x, new


def kernel(x_prompt, x_sample, state_hgrn, state_rglru_h, state_rglru_conv, cache_mla_latent, cache_mla_rope, state_conformer_conv, meta_tokens, norm_pre, norm_post, a_w_in, a_lb_logits, a_norm_g, a_w_out, b_w_in, b_conv_w, b_conv_b, b_wa, b_ba, b_wx, b_bx, b_lambda, b_w_out, c_w_in, c_q_norm, c_kv_norm, c_w_uq, c_w_uk, c_w_uv, c_w_out, d_w_in, d_conv_w, d_conv_b, d_ln_g, d_ln_b, d_w_out):
    assert norm_pre.shape[0] == 4, "one layer of each mixer type"
    bf = lambda a: a.astype(BF16)

    c_in = c_w_in[0]
    i1, i2, i3 = C_Q_LORA, C_Q_LORA + C_KV_LORA, C_Q_LORA + C_KV_LORA + C_ROPE
    half = C_ROPE // 2
    k_pe_cols = c_in[:, i2:i3]
    k_pe_sw = jnp.concatenate([k_pe_cols[:, half:], k_pe_cols[:, :half]], axis=1)
    c_in_perm = jnp.concatenate([c_in[:, i3:], c_in[:, :i3], k_pe_sw], axis=1)
    uq = c_w_uq[0].reshape(C_Q_LORA, C_HEADS, C_NOPE + C_ROPE)
    uq_pe = uq[:, :, C_NOPE:]
    uq_pe_sw = jnp.concatenate([uq_pe[..., half:], uq_pe[..., :half]], axis=-1)

    w = {
        "norm_pre": norm_pre, "norm_post": norm_post,
        "w_in": [bf(a_w_in[0]), bf(b_w_in[0]), bf(c_in_perm), bf(d_w_in[0])],
        "w_out": [bf(a_w_out[0]), bf(b_w_out[0]), bf(c_w_out[0]), bf(d_w_out[0])],
        "a_lb_logits": a_lb_logits, "a_norm_g": a_norm_g[0],
        "b_conv_w": b_conv_w[0], "b_conv_b": b_conv_b[0], "b_wa": bf(b_wa[0]), "b_ba": b_ba[0],
        "b_wx": bf(b_wx[0]), "b_bx": b_bx[0], "b_lambda": b_lambda[0],
        "c_q_norm": c_q_norm[0], "c_kv_norm": c_kv_norm[0],
        "c_w_nope": bf(uq[:, :, :C_NOPE].reshape(C_Q_LORA, C_HEADS * C_NOPE)),
        "c_w_pe": bf(uq_pe.reshape(C_Q_LORA, C_HEADS * C_ROPE)),
        "c_w_pe_sw": bf(uq_pe_sw.reshape(C_Q_LORA, C_HEADS * C_ROPE)),
        "c_w_uk_h": bf(jnp.transpose(c_w_uk[0].reshape(C_KV_LORA, C_HEADS, C_NOPE), (1, 2, 0))),
        "c_w_uv_h": bf(jnp.transpose(c_w_uv[0].reshape(C_KV_LORA, C_HEADS, C_V), (1, 0, 2))),
        "d_conv_w": d_conv_w[0], "d_conv_b": d_conv_b[0], "d_ln_g": d_ln_g[0], "d_ln_b": d_ln_b[0],
    }

    bp = x_prompt.shape[0]
    dt = x_prompt.dtype

    st_m = {"hgrn": jnp.zeros((1, A_HEADS, A_D, A_D), dt), "rg_h": jnp.zeros((1, E_WIDTH), dt),
            "rg_conv": jnp.zeros((1, B_CONV - 1, E_WIDTH), dt), "mla_c": None, "mla_pe": None,
            "conf": jnp.zeros((1, D_CONV - 1, E_WIDTH), dt)}
    _, new_m = _trunk(meta_tokens.astype(dt)[None], w, st_m, pos0=0, reset_first=True, chunked=False)

    st_p = {"hgrn": new_m["hgrn"], "rg_h": new_m["rg_h"], "rg_conv": new_m["rg_conv"],
            "mla_c": new_m["mla_c"], "mla_pe": new_m["mla_pe"], "conf": new_m["conf"]}
    yp, new_p = _trunk(x_prompt, w, st_p, pos0=N_META, reset_first=False, chunked=True)

    st_s = {"hgrn": state_hgrn[0], "rg_h": state_rglru_h[0], "rg_conv": state_rglru_conv[0],
            "mla_c": cache_mla_latent[0], "mla_pe": cache_mla_rope[0], "conf": state_conformer_conv[0]}
    ys, new_s = _trunk(x_sample, w, st_s, pos0=cache_mla_latent.shape[2], reset_first=False, chunked=False)

    def with_meta(meta_rows, rows):
        return jnp.concatenate([jnp.broadcast_to(meta_rows, (bp,) + meta_rows.shape[1:]), rows], axis=1)

    return (yp, ys,
            new_p["hgrn"][None], new_s["hgrn"][None],
            new_p["rg_h"][None], new_s["rg_h"][None],
            new_p["rg_conv"][None], new_s["rg_conv"][None],
            with_meta(new_m["mla_c"], new_p["mla_c"])[None], new_s["mla_c"][None],
            with_meta(new_m["mla_pe"], new_p["mla_pe"])[None], new_s["mla_pe"][None],
            new_p["conf"][None], new_s["conf"][None])
```

```python
import functools

import jax
import jax.numpy as jnp
from jax import lax
from jax.experimental import pallas as pl
from jax.experimental.pallas import tpu as pltpu

F32 = jnp.float32
BF16 = jnp.bfloat16

EPS = 1e-6
D_MODEL = 1024
E_WIDTH = 2048
N_META = 16
CHUNK = 64
A_HEADS = 16
A_D = 128
A_SUB = 16
A_CHUNK = 128
B_BLOCKS = 8
B_BS = 256
B_CONV = 4
B_C = 8.0
C_HEADS = 16
C_NOPE = 128
C_ROPE = 64
C_V = 128
C_Q_LORA = 512
C_KV_LORA = 256
C_SCALE = (C_NOPE + C_ROPE) ** -0.5
LOG2E = 1.4426950408889634
ROPE_BASE = 10000.0
KEY_BLOCK = 256
ATTN_TILES = 4
D_CONV = 31
D_HIST = 32

VMEM_LIMIT = 48 * 1024 * 1024


def _params(sem):
    return pltpu.CompilerParams(dimension_semantics=sem, vmem_limit_bytes=VMEM_LIMIT)


def _dot(a, b):
    return jnp.dot(a, b, preferred_element_type=F32)


def _dot_nt(a, b):
    return lax.dot_general(a, b, (((1,), (1,)), ((), ())), preferred_element_type=F32)


def _sigmoid(x):
    return jax.nn.sigmoid(x)


def _silu(x):
    return x * jax.nn.sigmoid(x)


def _rms(x, g):
    return x * lax.rsqrt(jnp.mean(x * x, axis=-1, keepdims=True) + EPS) * g


def _out_res_kernel(h_ref, w_ref, g_ref, x_ref, o_ref):
    y = _dot(h_ref[...], w_ref[...])
    o_ref[...] = x_ref[...] + _rms(y, g_ref[...])


def _out_res(h2d, w, g, x2d):
    m, e = h2d.shape
    d = w.shape[1]
    tm = min(m, 1024)
    return pl.pallas_call(
        _out_res_kernel,
        grid=(m // tm,),
        in_specs=[pl.BlockSpec((tm, e), lambda i: (i, 0)),
                  pl.BlockSpec((e, d), lambda i: (0, 0)),
                  pl.BlockSpec((1, d), lambda i: (0, 0)),
                  pl.BlockSpec((tm, d), lambda i: (i, 0))],
        out_specs=pl.BlockSpec((tm, d), lambda i: (i, 0)),
        out_shape=jax.ShapeDtypeStruct((m, d), F32),
        compiler_params=_params(("parallel",)),
        name="out_res",
    )(h2d, w, g.reshape(1, d), x2d)


def _cumsum_rows(x):
    n = x.shape[0]
    row = lax.broadcasted_iota(jnp.int32, x.shape, 0)
    s = 1
    while s < n:
        x = x + jnp.where(row >= s, pltpu.roll(x, s, axis=0), 0.0)
        s *= 2
    return x


def _piecewise_rows(cum, seg, pick):
    c, l = cum.shape
    pieces = []
    for i in range(c // seg):
        r = pick(i)
        if r is None:
            pieces.append(jnp.zeros((seg, l), F32))
        else:
            pieces.append(jnp.broadcast_to(cum[r:r + 1, :], (seg, l)))
    return pieces[0] if len(pieces) == 1 else jnp.concatenate(pieces, axis=0)


def _mla_q_kernel(x_ref, gpre_ref, win_ref, cos_ref, sin_ref, qn_ref, kvn_ref, wn_ref, wp_ref, wps_ref, wuk_ref,
                  g_ref, qa_ref, qp_ref, ckv_ref, kpe_ref):
    u = _rms(x_ref[0], gpre_ref[...]).astype(BF16)
    g_ref[0] = _dot(u, win_ref[:, :E_WIDTH])
    q_lat = _dot(u, win_ref[:, E_WIDTH:E_WIDTH + C_Q_LORA])
    kv = _dot(u, win_ref[:, E_WIDTH + C_Q_LORA:])
    ql = _rms(q_lat, qn_ref[...]).astype(BF16)
    q_nope = _dot(ql, wn_ref[...])
    q_pe = _dot(ql, wp_ref[...])
    q_sw = _dot(ql, wps_ref[...])
    cos = cos_ref[...]
    sin = sin_ref[...]
    for j in range(C_HEADS // 2):
        sl = slice(j * 128, (j + 1) * 128)
        r = (q_pe[:, sl] * cos + q_sw[:, sl] * sin).astype(BF16)
        qp_ref[0, 2 * j] = r[:, :C_ROPE]
        qp_ref[0, 2 * j + 1] = r[:, C_ROPE:]
    for h in range(C_HEADS):
        qh = q_nope[:, h * C_NOPE:(h + 1) * C_NOPE].astype(BF16)
        qa_ref[0, h] = _dot(qh, wuk_ref[h]).astype(BF16)
    ckv_ref[0] = _rms(kv[:, :C_KV_LORA], kvn_ref[...])
    k_pe = kv[:, C_KV_LORA:C_KV_LORA + C_ROPE]
    k_sw = kv[:, C_KV_LORA + C_ROPE:C_KV_LORA + 2 * C_ROPE]
    kpe_ref[0] = k_pe * cos[:, :C_ROPE] + k_sw * sin[:, :C_ROPE]


def _mla_q(x, g_pre, w_in, cos, sin, q_norm, kv_norm, w_nope, w_pe, w_pe_sw, w_uk_h):
    b, t, d = x.shape
    tm = min(t, 256)
    full = lambda shape: pl.BlockSpec(shape, lambda bi, ti: (0,) * len(shape))
    return pl.pallas_call(
        _mla_q_kernel,
        grid=(b, t // tm),
        in_specs=[pl.BlockSpec((1, tm, d), lambda bi, ti: (bi, ti, 0)),
                  full((1, d)), _resident(w_in.shape),
                  pl.BlockSpec((tm, 128), lambda bi, ti: (ti, 0)),
                  pl.BlockSpec((tm, 128), lambda bi, ti: (ti, 0)),
                  full((1, C_Q_LORA)), full((1, C_KV_LORA)),
                  full((C_Q_LORA, C_HEADS * C_NOPE)), full((C_Q_LORA, C_HEADS * C_ROPE)),
                  full((C_Q_LORA, C_HEADS * C_ROPE)), full((C_HEADS, C_NOPE, C_KV_LORA))],
        out_specs=[pl.BlockSpec((1, tm, E_WIDTH), lambda bi, ti: (bi, ti, 0)),
                   pl.BlockSpec((1, C_HEADS, tm, C_KV_LORA), lambda bi, ti: (bi, 0, ti, 0)),
                   pl.BlockSpec((1, C_HEADS, tm, C_ROPE), lambda bi, ti: (bi, 0, ti, 0)),
                   pl.BlockSpec((1, tm, C_KV_LORA), lambda bi, ti: (bi, ti, 0)),
                   pl.BlockSpec((1, tm, C_ROPE), lambda bi, ti: (bi, ti, 0))],
        out_shape=[jax.ShapeDtypeStruct((b, t, E_WIDTH), F32),
                   jax.ShapeDtypeStruct((b, C_HEADS, t, C_KV_LORA), BF16),
                   jax.ShapeDtypeStruct((b, C_HEADS, t, C_ROPE), BF16),
                   jax.ShapeDtypeStruct((b, t, C_KV_LORA), F32),
                   jax.ShapeDtypeStruct((b, t, C_ROPE), F32)],
        compiler_params=_params(("parallel", "arbitrary")),
        name="mla_q",
    )(x, g_pre.reshape(1, d), w_in, cos, sin, q_norm.reshape(1, -1), kv_norm.reshape(1, -1), w_nope, w_pe, w_pe_sw,
      w_uk_h)


def _mla_attn_kernel(qa_ref, qp_ref, kc_ref, kct_ref, kp_ref, g_ref, wuv_ref, h_ref, acc_s, s_s, p_s, o_s, *,
                     tq, nt, n_prefix, n_total, chunked):
    for k in range(nt):
        _mla_attn_tile(pl.program_id(1) * nt + k, slice(k * tq, (k + 1) * tq), qa_ref, qp_ref, kc_ref, kct_ref,
                       kp_ref, o_s.at[k], acc_s, s_s, p_s, tq=tq, n_prefix=n_prefix, n_total=n_total,
                       chunked=chunked)
    for h in range(C_HEADS):
        o_h = jnp.concatenate([o_s[k, h * tq:(h + 1) * tq, :] for k in range(nt)], axis=0) if nt > 1 \
            else o_s[0, h * tq:(h + 1) * tq, :]
        sl = slice(h * C_V, (h + 1) * C_V)
        h_ref[0, :, sl] = (_dot(o_h, wuv_ref[h]) * _silu(g_ref[0, :, sl])).astype(BF16)


def _mla_attn_tile(ci, rows, qa_ref, qp_ref, kc_ref, kct_ref, kp_ref, o_ref, acc_s, s_s, p_s, *,
                   tq, n_prefix, n_total, chunked):
    r = C_HEADS * tq
    nv = (n_prefix + tq * (ci + 1)) if chunked else n_total
    kb = s_s.shape[0]
    nblk = (nv + kb - 1) // kb
    qa = qa_ref[0, :, rows, :].reshape(r, C_KV_LORA)
    qp = qp_ref[0, :, rows, :].reshape(r, C_ROPE)
    acc_s[...] = jnp.zeros((C_KV_LORA, r), F32)
    c2 = C_SCALE * LOG2E

    def scores(j):
        off = pl.multiple_of(j * kb, kb)
        return _dot_nt(kc_ref[0, pl.ds(off, kb), :], qa) + _dot_nt(kp_ref[0, pl.ds(off, kb), :], qp)

    def add_pv(j, alpha):
        off = pl.multiple_of(j * kb, kb)
        acc_s[...] = alpha * acc_s[...] + _dot(kct_ref[0, :, pl.ds(off, kb)], p_s[...])

    s_s[...] = scores(0)

    def trip(j, carry, first=False, prefetch=True):
        m_old, l_old, alpha_prev = carry
        s = s_s[...]
        if not first:
            add_pv(j - 1, alpha_prev)
        if prefetch:
            s_s[...] = scores(jnp.minimum(j + 1, kc_ref.shape[1] // kb - 1) if first else j + 1)
        key = j * kb + lax.broadcasted_iota(jnp.int32, (kb, 1), 0)
        s = jnp.where(key < nv, s, -jnp.inf)
        m_new = jnp.maximum(m_old, jnp.max(s, axis=0, keepdims=True))
        alpha = jnp.exp2((m_old - m_new) * c2)
        p = jnp.exp2((s - m_new) * c2)
        l_new = alpha * l_old + jnp.sum(p, axis=0, keepdims=True)
        p_s[...] = p.astype(BF16)
        return m_new, l_new, alpha

    init = (jnp.full((1, r), -jnp.inf, F32), jnp.zeros((1, r), F32), jnp.ones((1, r), F32))
    carry = trip(0, init, first=True)
    carry = lax.fori_loop(1, nblk - 1, trip, carry)
    carry = lax.cond(nblk >= 2, lambda cr: trip(nblk - 1, cr, prefetch=False), lambda cr: cr, carry)
    _, l_fin, alpha_fin = carry
    add_pv(nblk - 1, alpha_fin)
    o_ref[...] = (acc_s[...] / l_fin).T.astype(BF16)


def _mla_attn(qa, qp, kc, kp, proj, w_uv_h, n_prefix, chunked):
    b, _, t, _ = qa.shape
    tq = min(t, CHUNK)
    nt = ATTN_TILES if t % (ATTN_TILES * tq) == 0 else 1
    tb = nt * tq
    n_total = n_prefix + t
    kb = KEY_BLOCK
    kc, kp = _pad_keys(kc, kb), _pad_keys(kp, kb)
    tk = kc.shape[1]
    r = C_HEADS * tq
    kct = jnp.swapaxes(kc, 1, 2)
    kern = functools.partial(_mla_attn_kernel, tq=tq, nt=nt, n_prefix=n_prefix, n_total=n_total, chunked=chunked)
    return pl.pallas_call(
        kern,
        grid=(b, t // tb),
        in_specs=[pl.BlockSpec((1, C_HEADS, tb, C_KV_LORA), lambda bi, ci: (bi, 0, ci, 0)),
                  pl.BlockSpec((1, C_HEADS, tb, C_ROPE), lambda bi, ci: (bi, 0, ci, 0)),
                  pl.BlockSpec((1, tk, C_KV_LORA), lambda bi, ci: (bi, 0, 0)),
                  pl.BlockSpec((1, C_KV_LORA, tk), lambda bi, ci: (bi, 0, 0)),
                  pl.BlockSpec((1, tk, C_ROPE), lambda bi, ci: (bi, 0, 0)),
                  pl.BlockSpec((1, tb, E_WIDTH), lambda bi, ci: (bi, ci, 0)),
                  pl.BlockSpec((C_HEADS, C_KV_LORA, C_V), lambda bi, ci: (0, 0, 0))],
        out_specs=pl.BlockSpec((1, tb, E_WIDTH), lambda bi, ci: (bi, ci, 0)),
        out_shape=jax.ShapeDtypeStruct((b, t, E_WIDTH), BF16),
        scratch_shapes=[pltpu.VMEM((C_KV_LORA, r), F32), pltpu.VMEM((kb, r), F32), pltpu.VMEM((kb, r), BF16),
                        pltpu.VMEM((nt, r, C_KV_LORA), BF16)],
        compiler_params=_params(("parallel", "arbitrary")),
        name="mla_attn",
    )(qa, qp, kc, kct, kp, proj, w_uv_h)


def _resident(shape):
    return pl.BlockSpec(shape, lambda *_: (0,) * len(shape), pipeline_mode=pl.Buffered(1))


def _hgrn_layer_kernel(x_ref, gpre_ref, wq_ref, wf_ref, wi_ref, wg_ref, lbl_ref, ng_ref, s0_ref, h_ref, sout_ref,
                       st_ref, pj_s, *, c, nch, hb, lb_row):
    t = pl.program_id(2)

    @pl.when(t == 0)
    def _():
        for h in range(hb):
            st_ref[h] = s0_ref[0, h].T

    lg = lbl_ref[...]
    e = jnp.exp(lg - jnp.max(lg, axis=0, keepdims=True))
    lb_all = jnp.sum(e[:lb_row + 1], axis=0, keepdims=True) / jnp.sum(e, axis=0, keepdims=True)

    rowi = lax.broadcasted_iota(jnp.int32, (c, c), 0)
    coli = lax.broadcasted_iota(jnp.int32, (c, c), 1)
    diag_mask = (rowi // A_SUB == coli // A_SUB) & (coli <= rowi)
    halves = []
    s = A_SUB
    while 2 * s <= c:
        halves.append(s)
        s *= 2
    row1 = lax.broadcasted_iota(jnp.int32, (c, A_D), 0)

    u = _rms(x_ref[0], gpre_ref[...]).astype(BF16)
    w_refs = (wq_ref, wf_ref, wi_ref, wg_ref)

    def project(hp, part):
        pj_s[hp, part] = _dot(u, w_refs[part][:, hp * 2 * A_D:(hp + 1) * 2 * A_D])

    for part in range(4):
        project(0, part)
    slots = [(ch, hh) for ch in range(nch) for hh in range(2)]
    per_slot = -(-4 // len(slots))
    for hp in range(hb // 2):
        for si, (ch, hh) in enumerate(slots):
            if hp + 1 < hb // 2:
                for part in range(si * per_slot, min(4, (si + 1) * per_slot)):
                    project(hp + 1, part)
            h = 2 * hp + hh
            sl = slice(h * A_D, (h + 1) * A_D)
            ln = slice(hh * A_D, (hh + 1) * A_D)
            rows = slice(ch * c, (ch + 1) * c)
            lb = lb_all[:, sl]
            q = _silu(pj_s[hp, 0, rows, ln])
            fg = lb + (1.0 - lb) * _sigmoid(pj_s[hp, 1, rows, ln])
            k = 1.0 - fg
            v = pj_s[hp, 2, rows, ln].astype(BF16)
            cum = _cumsum_rows(jnp.log(fg))

            start = _piecewise_rows(cum, A_SUB, lambda i: None if i == 0 else i * A_SUB - 1)
            rel = cum - start
            att = jnp.where(diag_mask,
                            _dot_nt((q * jnp.exp(rel)).astype(BF16), (k * jnp.exp(-rel)).astype(BF16)), 0.0)
            for s in halves:
                mid = _piecewise_rows(cum, 2 * s, lambda i: i * 2 * s + s - 1)
                second = (row1 % (2 * s)) >= s
                w = jnp.exp(-jnp.abs(cum - mid))
                ql = jnp.where(second, q * w, 0.0).astype(BF16)
                kl = jnp.where(second, 0.0, k * w).astype(BF16)
                lev = _dot_nt(ql, kl)
                if 2 * s < c:
                    lev = jnp.where(rowi // (2 * s) == coli // (2 * s), lev, 0.0)
                att = att + lev

            st = st_ref[h]
            o = _dot(att.astype(BF16), v) + _dot_nt((q * jnp.exp(cum)).astype(BF16), st.astype(BF16))
            last = cum[c - 1:c, :]
            kd = (k * jnp.exp(last - cum)).astype(BF16)
            st_ref[h] = st * jnp.exp(last) + _dot(v.T, kd)

            on = _rms(o, ng_ref[...])
            h_ref[0, rows, sl] = (on * _silu(pj_s[hp, 3, rows, ln])).astype(BF16)

    @pl.when(t == pl.num_programs(2) - 1)
    def _():
        for h in range(hb):
            sout_ref[0, h] = st_ref[h].T


def _hgrn_layer(x, g_pre, w_in, lb_logits, lb_row, norm_g, s0):
    b, t, d = x.shape
    c = min(t, A_CHUNK)
    nch = 4 if t % (4 * c) == 0 else 1
    tt = nch * c
    hb = 8
    nhb = A_HEADS // hb
    w = hb * A_D
    bs = s0.shape[0]
    wcol = lambda part: pl.BlockSpec((d, w), lambda hi, bi, ti: (0, part * nhb + hi))
    kern = functools.partial(_hgrn_layer_kernel, c=c, nch=nch, hb=hb, lb_row=lb_row)
    return pl.pallas_call(
        kern,
        grid=(nhb, b, t // tt),
        in_specs=[pl.BlockSpec((1, tt, d), lambda hi, bi, ti: (bi, ti, 0)),
                  pl.BlockSpec((1, d), lambda hi, bi, ti: (0, 0)),
                  wcol(0), wcol(1), wcol(2), wcol(3),
                  pl.BlockSpec((lb_logits.shape[0], w), lambda hi, bi, ti: (0, hi)),
                  pl.BlockSpec((1, A_D), lambda hi, bi, ti: (0, 0)),
                  pl.BlockSpec((1, hb, A_D, A_D), lambda hi, bi, ti: (bi if bs > 1 else 0, hi, 0, 0))],
        out_specs=[pl.BlockSpec((1, tt, w), lambda hi, bi, ti: (bi, ti, hi)),
                   pl.BlockSpec((1, hb, A_D, A_D), lambda hi, bi, ti: (bi, hi, 0, 0))],
        out_shape=[jax.ShapeDtypeStruct((b, t, E_WIDTH), BF16),
                   jax.ShapeDtypeStruct((b, A_HEADS, A_D, A_D), F32)],
        scratch_shapes=[pltpu.VMEM((hb, A_D, A_D), F32), pltpu.VMEM((hb // 2, 4, tt, 2 * A_D), F32)],
        compiler_params=_params(("parallel", "parallel", "arbitrary")),
        name="hgrn_layer",
    )(x, g_pre.reshape(1, d), w_in, w_in, w_in, w_in, lb_logits, norm_g.reshape(1, A_D), s0)


def _rglru_layer_kernel(x_ref, gpre_ref, win_ref, cw_ref, cb_ref, wa_ref, ba_ref, wx_ref, bx_ref, lam_ref, h0_ref,
                        buf_ref, h_ref, hl_ref, nbuf_ref, ext_s, a_s, b_s, gb_s, hc_s, *, tt, reset_first):
    t = pl.program_id(1)
    hist = 8
    nh = B_CONV - 1

    @pl.when(t == 0)
    def _():
        ext_s[hist - nh:hist, :] = buf_ref[0]
        hc_s[...] = h0_ref[0]

    u = _rms(x_ref[0], gpre_ref[...]).astype(BF16)
    lam = lam_ref[...]
    sp = jnp.maximum(-lam, 0.0) + jnp.log1p(jnp.exp(-jnp.abs(lam)))
    row = lax.broadcasted_iota(jnp.int32, (tt, B_BS), 0)
    def project_x(n):
        ext_s[hist:hist + tt, n * B_BS:(n + 1) * B_BS] = _dot(u, win_ref[:, n * B_BS:(n + 1) * B_BS])

    def project_g(n):
        gb_s[:, n * B_BS:(n + 1) * B_BS] = _dot(u, win_ref[:, E_WIDTH + n * B_BS:E_WIDTH + (n + 1) * B_BS])

    project_x(0)
    project_g(0)
    for n in range(B_BLOCKS):
        sl = slice(n * B_BS, (n + 1) * B_BS)
        xn = cb_ref[:, sl] + cw_ref[0:1, sl] * ext_s[hist - nh:hist - nh + tt, sl]
        for k in range(1, B_CONV):
            xn = xn + cw_ref[k:k + 1, sl] * ext_s[hist - nh + k:hist - nh + k + tt, sl]
        if n + 1 < B_BLOCKS:
            project_x(n + 1)
        xnb = xn.astype(BF16)
        r = _sigmoid(_dot(xnb, wa_ref[n]) + ba_ref[:, sl])
        if n + 1 < B_BLOCKS:
            project_g(n + 1)
        ig = _sigmoid(_dot(xnb, wx_ref[n]) + bx_ref[:, sl])
        log_a = -B_C * r * sp[:, sl]
        a = jnp.exp(log_a)
        mult = jnp.sqrt(jnp.tanh(-log_a) * (1.0 + a * a))
        if reset_first:
            mult = jnp.where((row == 0) & (t == 0), 1.0, mult)
        a_s[:, sl] = a
        b_s[:, sl] = mult * ig * xn

    tail = ext_s[hist + tt - nh:hist + tt, :]
    nbuf_ref[0] = tail
    ext_s[hist - nh:hist, :] = tail

    row8 = lax.broadcasted_iota(jnp.int32, (8, E_WIDTH), 0)

    def body(g, hc):
        r0 = pl.multiple_of(g * 8, 8)
        a = a_s[pl.ds(r0, 8), :]
        b = b_s[pl.ds(r0, 8), :]
        for s in (1, 2, 4):
            m = row8 >= s
            b = jnp.where(m, a * pltpu.roll(b, s, axis=0) + b, b)
            a = jnp.where(m, a * pltpu.roll(a, s, axis=0), a)
        hrows = a * hc + b
        h_ref[0, pl.ds(r0, 8), :] = (hrows * _silu(gb_s[pl.ds(r0, 8), :])).astype(BF16)
        return hrows[7:8, :]

    hc = lax.fori_loop(0, tt // 8, body, hc_s[...], unroll=4 if tt % 32 == 0 else 1)
    hc_s[...] = hc
    hl_ref[0] = hc


def _rglru_layer(x, g_pre, w_in, conv_w, conv_b, wa, ba, wx, bx, lam, h0, buf, reset_first):
    b, t, d = x.shape
    tt = min(t, 256)
    bs = h0.shape[0]
    e = E_WIDTH
    row = lambda a: a.reshape(1, e)
    full = lambda shape: pl.BlockSpec(shape, lambda bi, ti: (0,) * len(shape))
    kern = functools.partial(_rglru_layer_kernel, tt=tt, reset_first=reset_first)
    return pl.pallas_call(
        kern,
        grid=(b, t // tt),
        in_specs=[pl.BlockSpec((1, tt, d), lambda bi, ti: (bi, ti, 0)),
                  full((1, d)), _resident((d, 2 * e)),
                  full((B_CONV, e)), full((1, e)),
                  full((B_BLOCKS, B_BS, B_BS)), full((1, e)),
                  full((B_BLOCKS, B_BS, B_BS)), full((1, e)), full((1, e)),
                  pl.BlockSpec((1, 1, e), lambda bi, ti: (bi if bs > 1 else 0, 0, 0)),
                  pl.BlockSpec((1, B_CONV - 1, e), lambda bi, ti: (bi if bs > 1 else 0, 0, 0))],
        out_specs=[pl.BlockSpec((1, tt, e), lambda bi, ti: (bi, ti, 0)),
                   pl.BlockSpec((1, 1, e), lambda bi, ti: (bi, 0, 0)),
                   pl.BlockSpec((1, B_CONV - 1, e), lambda bi, ti: (bi, 0, 0))],
        out_shape=[jax.ShapeDtypeStruct((b, t, e), BF16),
                   jax.ShapeDtypeStruct((b, 1, e), F32),
                   jax.ShapeDtypeStruct((b, B_CONV - 1, e), F32)],
        scratch_shapes=[pltpu.VMEM((8 + tt, e), F32), pltpu.VMEM((tt, e), F32), pltpu.VMEM((tt, e), F32),
                        pltpu.VMEM((tt, e), F32), pltpu.VMEM((1, e), F32)],
        compiler_params=_params(("parallel", "arbitrary")),
        name="rglru_layer",
    )(x, g_pre.reshape(1, d), w_in, conv_w, row(conv_b), wa, row(ba), wx, row(bx), row(lam), h0.reshape(bs, 1, e), buf)


def _conf_layer_kernel(x_ref, gpre_ref, win_ref, cw_ref, cb_ref, lg_ref, lb_ref, buf_ref, h_ref, nbuf_ref,
                       hist_s, win_s, sh_s, conv_s, gate_s, *, tt):
    t = pl.program_id(1)
    nhist = D_CONV - 1
    lead = D_HIST - nhist
    cw = 256
    nc = E_WIDTH // cw
    rc = min(tt, 64)
    nsh = tt + D_HIST - 8

    @pl.when(t == 0)
    def _():
        hist_s[0:lead, :] = jnp.zeros((lead, E_WIDTH), F32)
        hist_s[lead:D_HIST, :] = buf_ref[0]

    u = _rms(x_ref[0], gpre_ref[...]).astype(BF16)

    def project_stages(ci):
        cs = slice(ci * cw, (ci + 1) * cw)
        held = {}

        def glu_in():
            held["a"] = _dot(u, win_ref[:, cs])

        def glu_gate():
            b = _dot(u, win_ref[:, E_WIDTH + ci * cw:E_WIDTH + (ci + 1) * cw])
            win = win_s.at[ci % 2]
            win[0:D_HIST, :] = hist_s[:, cs]
            win[D_HIST:D_HIST + tt, :] = held["a"] * _sigmoid(b)

        def out_gate():
            gate_s[:, cs] = _dot(u, win_ref[:, 2 * E_WIDTH + ci * cw:2 * E_WIDTH + (ci + 1) * cw])

        return [glu_in, glu_gate, out_gate]

    for stage in project_stages(0):
        stage()
    nrc = tt // rc
    for ci in range(nc):
        pending = project_stages(ci + 1) if ci + 1 < nc else []
        per_slot = -(-len(pending) // nrc) if pending else 0
        cs = slice(ci * cw, (ci + 1) * cw)
        win = win_s.at[ci % 2]
        sh = sh_s.at[ci % 2]
        hist_s[:, cs] = win[tt:tt + D_HIST, :]
        for r in range(1, 8):
            sh[r - 1] = win[r:r + nsh, :]
        for ri, r0 in enumerate(range(0, tt, rc)):
            for stage in pending[ri * per_slot:(ri + 1) * per_slot]:
                stage()
            acc = jnp.broadcast_to(cb_ref[:, cs], (rc, cw))
            for k in range(D_CONV):
                j8, r = divmod(lead + k, 8)
                lo = 8 * j8 + r0
                taps = win[lo:lo + rc, :] if r == 0 else sh[r - 1, lo:lo + rc, :]
                acc = acc + cw_ref[k:k + 1, cs] * taps
            conv_s[r0:r0 + rc, cs] = acc

    nbuf_ref[0] = hist_s[lead:D_HIST, :]

    rn = 16

    def norm_body(i, carry):
        r0 = pl.multiple_of(i * rn, rn)
        c = conv_s[pl.ds(r0, rn), :]
        mu = jnp.mean(c, axis=-1, keepdims=True)
        xc = c - mu
        var = jnp.mean(xc * xc, axis=-1, keepdims=True)
        y = _silu(xc * lax.rsqrt(var + EPS) * lg_ref[...] + lb_ref[...])
        h_ref[0, pl.ds(r0, rn), :] = (y * _silu(gate_s[pl.ds(r0, rn), :])).astype(BF16)
        return carry

    lax.fori_loop(0, tt // rn, norm_body, 0, unroll=min(4, tt // rn))


def _conf_layer(x, g_pre, w_in, conv_w, conv_b, ln_g, ln_b, buf):
    b, t, d = x.shape
    tt = min(t, 256)
    bs = buf.shape[0]
    e = E_WIDTH
    row = lambda a: a.reshape(1, e)
    full = lambda shape: pl.BlockSpec(shape, lambda bi, ti: (0,) * len(shape))
    kern = functools.partial(_conf_layer_kernel, tt=tt)
    return pl.pallas_call(
        kern,
        grid=(b, t // tt),
        in_specs=[pl.BlockSpec((1, tt, d), lambda bi, ti: (bi, ti, 0)),
                  full((1, d)), _resident((d, 3 * e)),
                  full((D_CONV, e)), full((1, e)), full((1, e)), full((1, e)),
                  pl.BlockSpec((1, D_CONV - 1, e), lambda bi, ti: (bi if bs > 1 else 0, 0, 0))],
        out_specs=[pl.BlockSpec((1, tt, e), lambda bi, ti: (bi, ti, 0)),
                   pl.BlockSpec((1, D_CONV - 1, e), lambda bi, ti: (bi, 0, 0))],
        out_shape=[jax.ShapeDtypeStruct((b, t, e), BF16),
                   jax.ShapeDtypeStruct((b, D_CONV - 1, e), F32)],
        scratch_shapes=[pltpu.VMEM((D_HIST, e), F32), pltpu.VMEM((2, D_HIST + tt, 256), F32),
                        pltpu.VMEM((2, 7, tt + D_HIST - 8, 256), F32), pltpu.VMEM((tt, e), F32),
                        pltpu.VMEM((tt, e), F32)],
        compiler_params=_params(("parallel", "arbitrary")),
        name="conf_layer",
    )(x, g_pre.reshape(1, d), w_in, conv_w, row(conv_b), row(ln_g), row(ln_b), buf)


def _rope_tables(pos0, t):
    inv = 1.0 / (ROPE_BASE ** (jnp.arange(0, C_ROPE, 2, dtype=F32) / C_ROPE))
    ang = (pos0 + jnp.arange(t, dtype=jnp.int32)).astype(F32)[:, None] * inv[None, :]
    cos, sin = jnp.cos(ang), jnp.sin(ang)
    return jnp.tile(cos, (1, 4)), jnp.tile(jnp.concatenate([-sin, sin], axis=-1), (1, 2))


def _pad_keys(k, kb):
    tk = k.shape[1]
    tkp = -(-tk // kb) * kb
    return jnp.pad(k.astype(BF16), ((0, 0), (0, tkp - tk), (0, 0)))


def _trunk(x, w, st, *, pos0, reset_first, chunked):
    b, t, d = x.shape
    m = b * t
    new = {}

    def layer(x, idx, mix):
        h = mix(x, w["norm_pre"][idx], w["w_in"][idx])
        return _out_res(h.reshape(m, E_WIDTH), w["w_out"][idx], w["norm_post"][idx], x.reshape(m, d)).reshape(b, t, d)

    def mix_a(x, g_pre, w_in):
        h, new["hgrn"] = _hgrn_layer(x, g_pre, w_in, w["a_lb_logits"], 0, w["a_norm_g"], st["hgrn"])
        return h

    def mix_b(x, g_pre, w_in):
        h, hl, new["rg_conv"] = _rglru_layer(x, g_pre, w_in, w["b_conv_w"], w["b_conv_b"], w["b_wa"], w["b_ba"],
                                             w["b_wx"], w["b_bx"], w["b_lambda"], st["rg_h"], st["rg_conv"],
                                             reset_first)
        new["rg_h"] = hl.reshape(b, E_WIDTH)
        return h

    def mix_c(x, g_pre, w_in):
        cos, sin = _rope_tables(pos0, t)
        gate, qa, qp, ckv, kpe = _mla_q(x, g_pre, w_in, cos, sin, w["c_q_norm"], w["c_kv_norm"], w["c_w_nope"],
                                        w["c_w_pe"], w["c_w_pe_sw"], w["c_w_uk_h"])
        new["mla_c"], new["mla_pe"] = ckv, kpe
        kc, kp = ckv, kpe
        n_prefix = 0
        if st["mla_c"] is not None:
            pc, pp = st["mla_c"], st["mla_pe"]
            n_prefix = pc.shape[1]
            pc = jnp.broadcast_to(pc, (b,) + pc.shape[1:])
            pp = jnp.broadcast_to(pp, (b,) + pp.shape[1:])
            kc = jnp.concatenate([pc.astype(BF16), ckv.astype(BF16)], axis=1)
            kp = jnp.concatenate([pp.astype(BF16), kpe.astype(BF16)], axis=1)
        return _mla_attn(qa, qp, kc, kp, gate, w["c_w_uv_h"], n_prefix, chunked)

    def mix_d(x, g_pre, w_in):
        h, new["conf"] = _conf_layer(x, g_pre, w_in, w["d_conv_w"], w["d_conv_b"], w["d_ln_g"], w["d_ln_b"],
                                     st["conf"])
        return h

    for idx, mix in enumerate((mix_a, mix_b, mix_c, mix_d)):
        x = layer(x, idx, mix)
    return x, new


def kernel(x_prompt, x_sample, state_hgrn, state_rglru_h, state_rglru_conv, cache_mla_latent, cache_mla_rope, state_conformer_conv, meta_tokens, norm_pre, norm_post, a_w_in, a_lb_logits, a_norm_g, a_w_out, b_w_in, b_conv_w, b_conv_b, b_wa, b_ba, b_wx, b_bx, b_lambda, b_w_out, c_w_in, c_q_norm, c_kv_norm, c_w_uq, c_w_uk, c_w_uv, c_w_out, d_w_in, d_conv_w, d_conv_b, d_ln_g, d_ln_b, d_w_out):
    assert norm_pre.shape[0] == 4, "one layer of each mixer type"
    bf = lambda a: a.astype(BF16)

    c_in = c_w_in[0]
    i1, i2, i3 = C_Q_LORA, C_Q_LORA + C_KV_LORA, C_Q_LORA + C_KV_LORA + C_ROPE
    half = C_ROPE // 2
    k_pe_cols = c_in[:, i2:i3]
    k_pe_sw = jnp.concatenate([k_pe_cols[:, half:], k_pe_cols[:, :half]], axis=1)
    c_in_perm = jnp.concatenate([c_in[:, i3:], c_in[:, :i3], k_pe_sw], axis=1)
    uq = c_w_uq[0].reshape(C_Q_LORA, C_HEADS, C_NOPE + C_ROPE)
    uq_pe = uq[:, :, C_NOPE:]
    uq_pe_sw = jnp.concatenate([uq_pe[..., half:], uq_pe[..., :half]], axis=-1)

    w = {
        "norm_pre": norm_pre, "norm_post": norm_post,
        "w_in": [bf(a_w_in[0]), bf(b_w_in[0]), bf(c_in_perm), bf(d_w_in[0])],
        "w_out": [bf(a_w_out[0]), bf(b_w_out[0]), bf(c_w_out[0]), bf(d_w_out[0])],
        "a_lb_logits": a_lb_logits, "a_norm_g": a_norm_g[0],
        "b_conv_w": b_conv_w[0], "b_conv_b": b_conv_b[0], "b_wa": bf(b_wa[0]), "b_ba": b_ba[0],
        "b_wx": bf(b_wx[0]), "b_bx": b_bx[0], "b_lambda": b_lambda[0],
        "c_q_norm": c_q_norm[0], "c_kv_norm": c_kv_norm[0],
        "c_w_nope": bf(uq[:, :, :C_NOPE].reshape(C_Q_LORA, C_HEADS * C_NOPE)),
        "c_w_pe": bf(uq_pe.reshape(C_Q_LORA, C_HEADS * C_ROPE)),
        "c_w_pe_sw": bf(uq_pe_sw.reshape(C_Q_LORA, C_HEADS * C_ROPE)),
        "c_w_uk_h": bf(jnp.transpose(c_w_uk[0].reshape(C_KV_LORA, C_HEADS, C_NOPE), (1, 2, 0))),
        "c_w_uv_h": bf(jnp.transpose(c_w_uv[0].reshape(C_KV_LORA, C_HEADS, C_V), (1, 0, 2))),
        "d_conv_w": d_conv_w[0], "d_conv_b": d_conv_b[0], "d_ln_g": d_ln_g[0], "d_ln_b": d_ln_b[0],
    }

    bp = x_prompt.shape[0]
    dt = x_prompt.dtype

    st_m = {"hgrn": jnp.zeros((1, A_HEADS, A_D, A_D), dt), "rg_h": jnp.zeros((1, E_WIDTH), dt),
            "rg_conv": jnp.zeros((1, B_CONV - 1, E_WIDTH), dt), "mla_c": None, "mla_pe": None,
            "conf": jnp.zeros((1, D_CONV - 1, E_WIDTH), dt)}
    _, new_m = _trunk(meta_tokens.astype(dt)[None], w, st_m, pos0=0, reset_first=True, chunked=False)

    st_p = {"hgrn": new_m["hgrn"], "rg_h": new_m["rg_h"], "rg_conv": new_m["rg_conv"],
            "mla_c": new_m["mla_c"], "mla_pe": new_m["mla_pe"], "conf": new_m["conf"]}
    yp, new_p = _trunk(x_prompt, w, st_p, pos0=N_META, reset_first=False, chunked=True)

    st_s = {"hgrn": state_hgrn[0], "rg_h": state_rglru_h[0], "rg_conv": state_rglru_conv[0],
            "mla_c": cache_mla_latent[0], "mla_pe": cache_mla_rope[0], "conf": state_conformer_conv[0]}
    ys, new_s = _trunk(x_sample, w, st_s, pos0=cache_mla_latent.shape[2], reset_first=False, chunked=False)

    def with_meta(meta_rows, rows):
        return jnp.concatenate([jnp.broadcast_to(meta_rows, (bp,) + meta_rows.shape[1:]), rows], axis=1)

    return (yp, ys,
            new_p["hgrn"][None], new_s["hgrn"][None],
            new_p["rg_h"][None], new_s["rg_h"][None],
            new_p["rg_conv"][None], new_s["rg_conv"][None],
            with_meta(new_m["mla_c"], new_p["mla_c"])[None], new_s["mla_c"][None],
            with_meta(new_m["mla_pe"], new_p["mla_pe"])[None], new_s["mla_pe"][None],
            new_p["conf"][None], new_s["conf"][None])
```

```python
import functools

import jax
import jax.numpy as jnp
from jax import lax
from jax.experimental import pallas as pl
from jax.experimental.pallas import tpu as pltpu

F32 = jnp.float32
BF16 = jnp.bfloat16

EPS = 1e-6
D_MODEL = 1024
E_WIDTH = 2048
N_META = 16
CHUNK = 64
A_HEADS = 16
A_D = 128
A_SUB = 16
A_CHUNK = 128
B_BLOCKS = 8
B_BS = 256
B_CONV = 4
B_C = 8.0
C_HEADS = 16
C_NOPE = 128
C_ROPE = 64
C_V = 128
C_Q_LORA = 512
C_KV_LORA = 256
C_SCALE = (C_NOPE + C_ROPE) ** -0.5
LOG2E = 1.4426950408889634
ROPE_BASE = 10000.0
KEY_BLOCK = 256
ATTN_TILES = 4
D_CONV = 31
D_HIST = 32

VMEM_LIMIT = 48 * 1024 * 1024


def _params(sem):
    return pltpu.CompilerParams(dimension_semantics=sem, vmem_limit_bytes=VMEM_LIMIT)


def _dot(a, b):
    return jnp.dot(a, b, preferred_element_type=F32)


def _dot_nt(a, b):
    return lax.dot_general(a, b, (((1,), (1,)), ((), ())), preferred_element_type=F32)


def _sigmoid(x):
    return jax.nn.sigmoid(x)


def _silu(x):
    return x * jax.nn.sigmoid(x)


def _rms(x, g):
    return x * lax.rsqrt(jnp.mean(x * x, axis=-1, keepdims=True) + EPS) * g


def _out_res_kernel(h_ref, w_ref, g_ref, x_ref, o_ref):
    y = _dot(h_ref[...], w_ref[...])
    o_ref[...] = x_ref[...] + _rms(y, g_ref[...])


def _out_res(h2d, w, g, x2d):
    m, e = h2d.shape
    d = w.shape[1]
    tm = min(m, 1024)
    return pl.pallas_call(
        _out_res_kernel,
        grid=(m // tm,),
        in_specs=[pl.BlockSpec((tm, e), lambda i: (i, 0)),
                  pl.BlockSpec((e, d), lambda i: (0, 0)),
                  pl.BlockSpec((1, d), lambda i: (0, 0)),
                  pl.BlockSpec((tm, d), lambda i: (i, 0))],
        out_specs=pl.BlockSpec((tm, d), lambda i: (i, 0)),
        out_shape=jax.ShapeDtypeStruct((m, d), F32),
        compiler_params=_params(("parallel",)),
        name="out_res",
    )(h2d, w, g.reshape(1, d), x2d)


def _cumsum_rows(x):
    n = x.shape[0]
    row = lax.broadcasted_iota(jnp.int32, x.shape, 0)
    s = 1
    while s < n:
        x = x + jnp.where(row >= s, pltpu.roll(x, s, axis=0), 0.0)
        s *= 2
    return x


def _piecewise_rows(cum, seg, pick):
    c, l = cum.shape
    pieces = []
    for i in range(c // seg):
        r = pick(i)
        if r is None:
            pieces.append(jnp.zeros((seg, l), F32))
        else:
            pieces.append(jnp.broadcast_to(cum[r:r + 1, :], (seg, l)))
    return pieces[0] if len(pieces) == 1 else jnp.concatenate(pieces, axis=0)


def _mla_q_kernel(x_ref, gpre_ref, win_ref, cos_ref, sin_ref, qn_ref, kvn_ref, wn_ref, wp_ref, wps_ref, wuk_ref,
                  g_ref, qa_ref, qp_ref, ckv_ref, kpe_ref):
    u = _rms(x_ref[0], gpre_ref[...]).astype(BF16)
    g_ref[0] = _dot(u, win_ref[:, :E_WIDTH])
    q_lat = _dot(u, win_ref[:, E_WIDTH:E_WIDTH + C_Q_LORA])
    kv = _dot(u, win_ref[:, E_WIDTH + C_Q_LORA:])
    ql = _rms(q_lat, qn_ref[...]).astype(BF16)
    q_nope = _dot(ql, wn_ref[...])
    q_pe = _dot(ql, wp_ref[...])
    q_sw = _dot(ql, wps_ref[...])
    cos = cos_ref[...]
    sin = sin_ref[...]
    for j in range(C_HEADS // 2):
        sl = slice(j * 128, (j + 1) * 128)
        r = (q_pe[:, sl] * cos + q_sw[:, sl] * sin).astype(BF16)
        qp_ref[0, 2 * j] = r[:, :C_ROPE]
        qp_ref[0, 2 * j + 1] = r[:, C_ROPE:]
    for h in range(C_HEADS):
        qh = q_nope[:, h * C_NOPE:(h + 1) * C_NOPE].astype(BF16)
        qa_ref[0, h] = _dot(qh, wuk_ref[h]).astype(BF16)
    ckv_ref[0] = _rms(kv[:, :C_KV_LORA], kvn_ref[...])
    k_pe = kv[:, C_KV_LORA:C_KV_LORA + C_ROPE]
    k_sw = kv[:, C_KV_LORA + C_ROPE:C_KV_LORA + 2 * C_ROPE]
    kpe_ref[0] = k_pe * cos[:, :C_ROPE] + k_sw * sin[:, :C_ROPE]


def _mla_q(x, g_pre, w_in, cos, sin, q_norm, kv_norm, w_nope, w_pe, w_pe_sw, w_uk_h):
    b, t, d = x.shape
    tm = min(t, 256)
    full = lambda shape: pl.BlockSpec(shape, lambda bi, ti: (0,) * len(shape))
    return pl.pallas_call(
        _mla_q_kernel,
        grid=(b, t // tm),
        in_specs=[pl.BlockSpec((1, tm, d), lambda bi, ti: (bi, ti, 0)),
                  full((1, d)), _resident(w_in.shape),
                  pl.BlockSpec((tm, 128), lambda bi, ti: (ti, 0)),
                  pl.BlockSpec((tm, 128), lambda bi, ti: (ti, 0)),
                  full((1, C_Q_LORA)), full((1, C_KV_LORA)),
                  full((C_Q_LORA, C_HEADS * C_NOPE)), full((C_Q_LORA, C_HEADS * C_ROPE)),
                  full((C_Q_LORA, C_HEADS * C_ROPE)), full((C_HEADS, C_NOPE, C_KV_LORA))],
        out_specs=[pl.BlockSpec((1, tm, E_WIDTH), lambda bi, ti: (bi, ti, 0)),
                   pl.BlockSpec((1, C_HEADS, tm, C_KV_LORA), lambda bi, ti: (bi, 0, ti, 0)),
                   pl.BlockSpec((1, C_HEADS, tm, C_ROPE), lambda bi, ti: (bi, 0, ti, 0)),
                   pl.BlockSpec((1, tm, C_KV_LORA), lambda bi, ti: (bi, ti, 0)),
                   pl.BlockSpec((1, tm, C_ROPE), lambda bi, ti: (bi, ti, 0))],
        out_shape=[jax.ShapeDtypeStruct((b, t, E_WIDTH), F32),
                   jax.ShapeDtypeStruct((b, C_HEADS, t, C_KV_LORA), BF16),
                   jax.ShapeDtypeStruct((b, C_HEADS, t, C_ROPE), BF16),
                   jax.ShapeDtypeStruct((b, t, C_KV_LORA), F32),
                   jax.ShapeDtypeStruct((b, t, C_ROPE), F32)],
        compiler_params=_params(("parallel", "arbitrary")),
        name="mla_q",
    )(x, g_pre.reshape(1, d), w_in, cos, sin, q_norm.reshape(1, -1), kv_norm.reshape(1, -1), w_nope, w_pe, w_pe_sw,
      w_uk_h)


def _mla_attn_kernel(qa_ref, qp_ref, kc_ref, kp_ref, g_ref, wuv_ref, h_ref, acc_s, s_s, p_s, o_s, *,
                     tq, nt, n_prefix, n_total, chunked):
    for k in range(nt):
        _mla_attn_tile(pl.program_id(1) * nt + k, slice(k * tq, (k + 1) * tq), qa_ref, qp_ref, kc_ref,
                       kp_ref, o_s.at[k], acc_s, s_s, p_s, tq=tq, n_prefix=n_prefix, n_total=n_total,
                       chunked=chunked)
    for h in range(C_HEADS):
        o_h = jnp.concatenate([o_s[k, h * tq:(h + 1) * tq, :] for k in range(nt)], axis=0) if nt > 1 \
            else o_s[0, h * tq:(h + 1) * tq, :]
        sl = slice(h * C_V, (h + 1) * C_V)
        h_ref[0, :, sl] = (_dot(o_h, wuv_ref[h]) * _silu(g_ref[0, :, sl])).astype(BF16)


def _mla_attn_tile(ci, rows, qa_ref, qp_ref, kc_ref, kp_ref, o_ref, acc_s, s_s, p_s, *,
                   tq, n_prefix, n_total, chunked):
    r = C_HEADS * tq
    nv = (n_prefix + tq * (ci + 1)) if chunked else n_total
    kb = s_s.shape[0]
    nblk = (nv + kb - 1) // kb
    qa = qa_ref[0, :, rows, :].reshape(r, C_KV_LORA)
    qp = qp_ref[0, :, rows, :].reshape(r, C_ROPE)
    acc_s[...] = jnp.zeros((C_KV_LORA, r), F32)
    c2 = C_SCALE * LOG2E

    def scores(j):
        off = pl.multiple_of(j * kb, kb)
        return _dot_nt(kc_ref[0, pl.ds(off, kb), :], qa) + _dot_nt(kp_ref[0, pl.ds(off, kb), :], qp)

    def add_pv(j, alpha):
        off = pl.multiple_of(j * kb, kb)
        pv = lax.dot_general(kc_ref[0, pl.ds(off, kb), :], p_s[...], (((0,), (0,)), ((), ())),
                             preferred_element_type=F32)
        acc_s[...] = alpha * acc_s[...] + pv

    s_s[...] = scores(0)

    def trip(j, carry, first=False, prefetch=True):
        m_old, l_old, alpha_prev = carry
        s = s_s[...]
        if not first:
            add_pv(j - 1, alpha_prev)
        if prefetch:
            s_s[...] = scores(jnp.minimum(j + 1, kc_ref.shape[1] // kb - 1) if first else j + 1)
        key = j * kb + lax.broadcasted_iota(jnp.int32, (kb, 1), 0)
        s = jnp.where(key < nv, s, -jnp.inf)
        m_new = jnp.maximum(m_old, jnp.max(s, axis=0, keepdims=True))
        alpha = jnp.exp2((m_old - m_new) * c2)
        p = jnp.exp2((s - m_new) * c2)
        l_new = alpha * l_old + jnp.sum(p, axis=0, keepdims=True)
        p_s[...] = p.astype(BF16)
        return m_new, l_new, alpha

    init = (jnp.full((1, r), -jnp.inf, F32), jnp.zeros((1, r), F32), jnp.ones((1, r), F32))
    carry = trip(0, init, first=True)
    carry = lax.fori_loop(1, nblk - 1, trip, carry)
    carry = lax.cond(nblk >= 2, lambda cr: trip(nblk - 1, cr, prefetch=False), lambda cr: cr, carry)
    _, l_fin, alpha_fin = carry
    add_pv(nblk - 1, alpha_fin)
    o_ref[...] = (acc_s[...] / l_fin).T.astype(BF16)


def _mla_attn(qa, qp, kc, kp, proj, w_uv_h, n_prefix, chunked):
    b, _, t, _ = qa.shape
    tq = min(t, CHUNK)
    nt = ATTN_TILES if t % (ATTN_TILES * tq) == 0 else 1
    tb = nt * tq
    n_total = n_prefix + t
    kb = KEY_BLOCK
    kc, kp = _pad_keys(kc, kb), _pad_keys(kp, kb)
    tk = kc.shape[1]
    r = C_HEADS * tq
    kern = functools.partial(_mla_attn_kernel, tq=tq, nt=nt, n_prefix=n_prefix, n_total=n_total, chunked=chunked)
    return pl.pallas_call(
        kern,
        grid=(b, t // tb),
        in_specs=[pl.BlockSpec((1, C_HEADS, tb, C_KV_LORA), lambda bi, ci: (bi, 0, ci, 0)),
                  pl.BlockSpec((1, C_HEADS, tb, C_ROPE), lambda bi, ci: (bi, 0, ci, 0)),
                  pl.BlockSpec((1, tk, C_KV_LORA), lambda bi, ci: (bi, 0, 0)),
                  pl.BlockSpec((1, tk, C_ROPE), lambda bi, ci: (bi, 0, 0)),
                  pl.BlockSpec((1, tb, E_WIDTH), lambda bi, ci: (bi, ci, 0)),
                  pl.BlockSpec((C_HEADS, C_KV_LORA, C_V), lambda bi, ci: (0, 0, 0))],
        out_specs=pl.BlockSpec((1, tb, E_WIDTH), lambda bi, ci: (bi, ci, 0)),
        out_shape=jax.ShapeDtypeStruct((b, t, E_WIDTH), BF16),
        scratch_shapes=[pltpu.VMEM((C_KV_LORA, r), F32), pltpu.VMEM((kb, r), F32), pltpu.VMEM((kb, r), BF16),
                        pltpu.VMEM((nt, r, C_KV_LORA), BF16)],
        compiler_params=_params(("parallel", "arbitrary")),
        name="mla_attn",
    )(qa, qp, kc, kp, proj, w_uv_h)


def _resident(shape):
    return pl.BlockSpec(shape, lambda *_: (0,) * len(shape), pipeline_mode=pl.Buffered(1))


def _hgrn_layer_kernel(x_ref, gpre_ref, wq_ref, wf_ref, wi_ref, wg_ref, lbl_ref, ng_ref, s0_ref, h_ref, sout_ref,
                       st_ref, pj_s, *, c, nch, hb, lb_row):
    t = pl.program_id(2)

    @pl.when(t == 0)
    def _():
        for h in range(hb):
            st_ref[h] = s0_ref[0, h].T

    lg = lbl_ref[...]
    e = jnp.exp(lg - jnp.max(lg, axis=0, keepdims=True))
    lb_all = jnp.sum(e[:lb_row + 1], axis=0, keepdims=True) / jnp.sum(e, axis=0, keepdims=True)

    rowi = lax.broadcasted_iota(jnp.int32, (c, c), 0)
    coli = lax.broadcasted_iota(jnp.int32, (c, c), 1)
    diag_mask = (rowi // A_SUB == coli // A_SUB) & (coli <= rowi)
    halves = []
    s = A_SUB
    while 2 * s <= c:
        halves.append(s)
        s *= 2
    row1 = lax.broadcasted_iota(jnp.int32, (c, A_D), 0)

    u = _rms(x_ref[0], gpre_ref[...]).astype(BF16)
    w_refs = (wq_ref, wf_ref, wi_ref, wg_ref)

    def project(hp, part):
        pj_s[hp, part] = _dot(u, w_refs[part][:, hp * 2 * A_D:(hp + 1) * 2 * A_D])

    for part in range(4):
        project(0, part)
    slots = [(ch, hh) for ch in range(nch) for hh in range(2)]
    per_slot = -(-4 // len(slots))
    for hp in range(hb // 2):
        for si, (ch, hh) in enumerate(slots):
            if hp + 1 < hb // 2:
                for part in range(si * per_slot, min(4, (si + 1) * per_slot)):
                    project(hp + 1, part)
            h = 2 * hp + hh
            sl = slice(h * A_D, (h + 1) * A_D)
            ln = slice(hh * A_D, (hh + 1) * A_D)
            rows = slice(ch * c, (ch + 1) * c)
            lb = lb_all[:, sl]
            q = _silu(pj_s[hp, 0, rows, ln])
            fg = lb + (1.0 - lb) * _sigmoid(pj_s[hp, 1, rows, ln])
            k = 1.0 - fg
            v = pj_s[hp, 2, rows, ln].astype(BF16)
            cum = _cumsum_rows(jnp.log(fg))

            start = _piecewise_rows(cum, A_SUB, lambda i: None if i == 0 else i * A_SUB - 1)
            rel = cum - start
            att = jnp.where(diag_mask,
                            _dot_nt((q * jnp.exp(rel)).astype(BF16), (k * jnp.exp(-rel)).astype(BF16)), 0.0)
            for s in halves:
                mid = _piecewise_rows(cum, 2 * s, lambda i: i * 2 * s + s - 1)
                second = (row1 % (2 * s)) >= s
                w = jnp.exp(-jnp.abs(cum - mid))
                ql = jnp.where(second, q * w, 0.0).astype(BF16)
                kl = jnp.where(second, 0.0, k * w).astype(BF16)
                lev = _dot_nt(ql, kl)
                if 2 * s < c:
                    lev = jnp.where(rowi // (2 * s) == coli // (2 * s), lev, 0.0)
                att = att + lev

            st = st_ref[h]
            o = _dot(att.astype(BF16), v) + _dot_nt((q * jnp.exp(cum)).astype(BF16), st.astype(BF16))
            last = cum[c - 1:c, :]
            kd = (k * jnp.exp(last - cum)).astype(BF16)
            st_ref[h] = st * jnp.exp(last) + _dot(v.T, kd)

            on = _rms(o, ng_ref[...])
            h_ref[0, rows, sl] = (on * _silu(pj_s[hp, 3, rows, ln])).astype(BF16)

    @pl.when(t == pl.num_programs(2) - 1)
    def _():
        for h in range(hb):
            sout_ref[0, h] = st_ref[h].T


def _hgrn_layer(x, g_pre, w_in, lb_logits, lb_row, norm_g, s0):
    b, t, d = x.shape
    c = min(t, A_CHUNK)
    nch = 4 if t % (4 * c) == 0 else 1
    tt = nch * c
    hb = 8
    nhb = A_HEADS // hb
    w = hb * A_D
    bs = s0.shape[0]
    wcol = lambda part: pl.BlockSpec((d, w), lambda hi, bi, ti: (0, part * nhb + hi))
    kern = functools.partial(_hgrn_layer_kernel, c=c, nch=nch, hb=hb, lb_row=lb_row)
    return pl.pallas_call(
        kern,
        grid=(nhb, b, t // tt),
        in_specs=[pl.BlockSpec((1, tt, d), lambda hi, bi, ti: (bi, ti, 0)),
                  pl.BlockSpec((1, d), lambda hi, bi, ti: (0, 0)),
                  wcol(0), wcol(1), wcol(2), wcol(3),
                  pl.BlockSpec((lb_logits.shape[0], w), lambda hi, bi, ti: (0, hi)),
                  pl.BlockSpec((1, A_D), lambda hi, bi, ti: (0, 0)),
                  pl.BlockSpec((1, hb, A_D, A_D), lambda hi, bi, ti: (bi if bs > 1 else 0, hi, 0, 0))],
        out_specs=[pl.BlockSpec((1, tt, w), lambda hi, bi, ti: (bi, ti, hi)),
                   pl.BlockSpec((1, hb, A_D, A_D), lambda hi, bi, ti: (bi, hi, 0, 0))],
        out_shape=[jax.ShapeDtypeStruct((b, t, E_WIDTH), BF16),
                   jax.ShapeDtypeStruct((b, A_HEADS, A_D, A_D), F32)],
        scratch_shapes=[pltpu.VMEM((hb, A_D, A_D), F32), pltpu.VMEM((hb // 2, 4, tt, 2 * A_D), F32)],
        compiler_params=_params(("parallel", "parallel", "arbitrary")),
        name="hgrn_layer",
    )(x, g_pre.reshape(1, d), w_in, w_in, w_in, w_in, lb_logits, norm_g.reshape(1, A_D), s0)


def _rglru_layer_kernel(x_ref, gpre_ref, win_ref, cw_ref, cb_ref, wa_ref, ba_ref, wx_ref, bx_ref, lam_ref, h0_ref,
                        buf_ref, h_ref, hl_ref, nbuf_ref, ext_s, a_s, b_s, gb_s, hc_s, *, tt, reset_first):
    t = pl.program_id(1)
    hist = 8
    nh = B_CONV - 1

    @pl.when(t == 0)
    def _():
        ext_s[hist - nh:hist, :] = buf_ref[0]
        hc_s[...] = h0_ref[0]

    u = _rms(x_ref[0], gpre_ref[...]).astype(BF16)
    lam = lam_ref[...]
    sp = jnp.maximum(-lam, 0.0) + jnp.log1p(jnp.exp(-jnp.abs(lam)))
    row = lax.broadcasted_iota(jnp.int32, (tt, B_BS), 0)
    def project_x(n):
        ext_s[hist:hist + tt, n * B_BS:(n + 1) * B_BS] = _dot(u, win_ref[:, n * B_BS:(n + 1) * B_BS])

    def project_g(n):
        gb_s[:, n * B_BS:(n + 1) * B_BS] = _dot(u, win_ref[:, E_WIDTH + n * B_BS:E_WIDTH + (n + 1) * B_BS])

    project_x(0)
    project_g(0)
    for n in range(B_BLOCKS):
        sl = slice(n * B_BS, (n + 1) * B_BS)
        xn = cb_ref[:, sl] + cw_ref[0:1, sl] * ext_s[hist - nh:hist - nh + tt, sl]
        for k in range(1, B_CONV):
            xn = xn + cw_ref[k:k + 1, sl] * ext_s[hist - nh + k:hist - nh + k + tt, sl]
        if n + 1 < B_BLOCKS:
            project_x(n + 1)
        xnb = xn.astype(BF16)
        r = _sigmoid(_dot(xnb, wa_ref[n]) + ba_ref[:, sl])
        if n + 1 < B_BLOCKS:
            project_g(n + 1)
        ig = _sigmoid(_dot(xnb, wx_ref[n]) + bx_ref[:, sl])
        log_a = -B_C * r * sp[:, sl]
        a = jnp.exp(log_a)
        mult = jnp.sqrt(jnp.tanh(-log_a) * (1.0 + a * a))
        if reset_first:
            mult = jnp.where((row == 0) & (t == 0), 1.0, mult)
        a_s[:, sl] = a
        b_s[:, sl] = mult * ig * xn

    tail = ext_s[hist + tt - nh:hist + tt, :]
    nbuf_ref[0] = tail
    ext_s[hist - nh:hist, :] = tail

    row8 = lax.broadcasted_iota(jnp.int32, (8, E_WIDTH), 0)

    def body(g, hc):
        r0 = pl.multiple_of(g * 8, 8)
        a = a_s[pl.ds(r0, 8), :]
        b = b_s[pl.ds(r0, 8), :]
        for s in (1, 2, 4):
            m = row8 >= s
            b = jnp.where(m, a * pltpu.roll(b, s, axis=0) + b, b)
            a = jnp.where(m, a * pltpu.roll(a, s, axis=0), a)
        hrows = a * hc + b
        h_ref[0, pl.ds(r0, 8), :] = (hrows * _silu(gb_s[pl.ds(r0, 8), :])).astype(BF16)
        return hrows[7:8, :]

    hc = lax.fori_loop(0, tt // 8, body, hc_s[...], unroll=4 if tt % 32 == 0 else 1)
    hc_s[...] = hc
    hl_ref[0] = hc


def _rglru_layer(x, g_pre, w_in, conv_w, conv_b, wa, ba, wx, bx, lam, h0, buf, reset_first):
    b, t, d = x.shape
    tt = min(t, 256)
    bs = h0.shape[0]
    e = E_WIDTH
    row = lambda a: a.reshape(1, e)
    full = lambda shape: pl.BlockSpec(shape, lambda bi, ti: (0,) * len(shape))
    kern = functools.partial(_rglru_layer_kernel, tt=tt, reset_first=reset_first)
    return pl.pallas_call(
        kern,
        grid=(b, t // tt),
        in_specs=[pl.BlockSpec((1, tt, d), lambda bi, ti: (bi, ti, 0)),
                  full((1, d)), _resident((d, 2 * e)),
                  full((B_CONV, e)), full((1, e)),
                  full((B_BLOCKS, B_BS, B_BS)), full((1, e)),
                  full((B_BLOCKS, B_BS, B_BS)), full((1, e)), full((1, e)),
                  pl.BlockSpec((1, 1, e), lambda bi, ti: (bi if bs > 1 else 0, 0, 0)),
                  pl.BlockSpec((1, B_CONV - 1, e), lambda bi, ti: (bi if bs > 1 else 0, 0, 0))],
        out_specs=[pl.BlockSpec((1, tt, e), lambda bi, ti: (bi, ti, 0)),
                   pl.BlockSpec((1, 1, e), lambda bi, ti: (bi, 0, 0)),
                   pl.BlockSpec((1, B_CONV - 1, e), lambda bi, ti: (bi, 0, 0))],
        out_shape=[jax.ShapeDtypeStruct((b, t, e), BF16),
                   jax.ShapeDtypeStruct((b, 1, e), F32),
                   jax.ShapeDtypeStruct((b, B_CONV - 1, e), F32)],
        scratch_shapes=[pltpu.VMEM((8 + tt, e), F32), pltpu.VMEM((tt, e), F32), pltpu.VMEM((tt, e), F32),
                        pltpu.VMEM((tt, e), F32), pltpu.VMEM((1, e), F32)],
        compiler_params=_params(("parallel", "arbitrary")),
        name="rglru_layer",
    )(x, g_pre.reshape(1, d), w_in, conv_w, row(conv_b), wa, row(ba), wx, row(bx), row(lam), h0.reshape(bs, 1, e), buf)


def _conf_layer_kernel(x_ref, gpre_ref, win_ref, cw_ref, cb_ref, lg_ref, lb_ref, buf_ref, h_ref, nbuf_ref,
                       hist_s, win_s, sh_s, conv_s, gate_s, *, tt):
    t = pl.program_id(1)
    nhist = D_CONV - 1
    lead = D_HIST - nhist
    cw = 256
    nc = E_WIDTH // cw
    rc = min(tt, 64)
    nsh = tt + D_HIST - 8

    @pl.when(t == 0)
    def _():
        hist_s[0:lead, :] = jnp.zeros((lead, E_WIDTH), F32)
        hist_s[lead:D_HIST, :] = buf_ref[0]

    u = _rms(x_ref[0], gpre_ref[...]).astype(BF16)

    def project_stages(ci):
        cs = slice(ci * cw, (ci + 1) * cw)
        held = {}

        def glu_in():
            held["a"] = _dot(u, win_ref[:, cs])

        def glu_gate():
            b = _dot(u, win_ref[:, E_WIDTH + ci * cw:E_WIDTH + (ci + 1) * cw])
            win = win_s.at[ci % 2]
            win[0:D_HIST, :] = hist_s[:, cs]
            win[D_HIST:D_HIST + tt, :] = held["a"] * _sigmoid(b)

        def out_gate():
            gate_s[:, cs] = _dot(u, win_ref[:, 2 * E_WIDTH + ci * cw:2 * E_WIDTH + (ci + 1) * cw])

        return [glu_in, glu_gate, out_gate]

    for stage in project_stages(0):
        stage()
    nrc = tt // rc
    for ci in range(nc):
        pending = project_stages(ci + 1) if ci + 1 < nc else []
        per_slot = -(-len(pending) // nrc) if pending else 0
        cs = slice(ci * cw, (ci + 1) * cw)
        win = win_s.at[ci % 2]
        sh = sh_s.at[ci % 2]
        hist_s[:, cs] = win[tt:tt + D_HIST, :]
        for r in range(1, 8):
            sh[r - 1] = win[r:r + nsh, :]
        for ri, r0 in enumerate(range(0, tt, rc)):
            for stage in pending[ri * per_slot:(ri + 1) * per_slot]:
                stage()
            acc = jnp.broadcast_to(cb_ref[:, cs], (rc, cw))
            for k in range(D_CONV):
                j8, r = divmod(lead + k, 8)
                lo = 8 * j8 + r0
                taps = win[lo:lo + rc, :] if r == 0 else sh[r - 1, lo:lo + rc, :]
                acc = acc + cw_ref[k:k + 1, cs] * taps
            conv_s[r0:r0 + rc, cs] = acc

    nbuf_ref[0] = hist_s[lead:D_HIST, :]

    rn = 16

    def norm_body(i, carry):
        r0 = pl.multiple_of(i * rn, rn)
        c = conv_s[pl.ds(r0, rn), :]
        mu = jnp.mean(c, axis=-1, keepdims=True)
        xc = c - mu
        var = jnp.mean(xc * xc, axis=-1, keepdims=True)
        y = _silu(xc * lax.rsqrt(var + EPS) * lg_ref[...] + lb_ref[...])
        h_ref[0, pl.ds(r0, rn), :] = (y * _silu(gate_s[pl.ds(r0, rn), :])).astype(BF16)
        return carry

    lax.fori_loop(0, tt // rn, norm_body, 0, unroll=min(4, tt // rn))


def _conf_layer(x, g_pre, w_in, conv_w, conv_b, ln_g, ln_b, buf):
    b, t, d = x.shape
    tt = min(t, 256)
    bs = buf.shape[0]
    e = E_WIDTH
    row = lambda a: a.reshape(1, e)
    full = lambda shape: pl.BlockSpec(shape, lambda bi, ti: (0,) * len(shape))
    kern = functools.partial(_conf_layer_kernel, tt=tt)
    return pl.pallas_call(
        kern,
        grid=(b, t // tt),
        in_specs=[pl.BlockSpec((1, tt, d), lambda bi, ti: (bi, ti, 0)),
                  full((1, d)), _resident((d, 3 * e)),
                  full((D_CONV, e)), full((1, e)), full((1, e)), full((1, e)),
                  pl.BlockSpec((1, D_CONV - 1, e), lambda bi, ti: (bi if bs > 1 else 0, 0, 0))],
        out_specs=[pl.BlockSpec((1, tt, e), lambda bi, ti: (bi, ti, 0)),
                   pl.BlockSpec((1, D_CONV - 1, e), lambda bi, ti: (bi, 0, 0))],
        out_shape=[jax.ShapeDtypeStruct((b, t, e), BF16),
                   jax.ShapeDtypeStruct((b, D_CONV - 1, e), F32)],
        scratch_shapes=[pltpu.VMEM((D_HIST, e), F32), pltpu.VMEM((2, D_HIST + tt, 256), F32),
                        pltpu.VMEM((2, 7, tt + D_HIST - 8, 256), F32), pltpu.VMEM((tt, e), F32),
                        pltpu.VMEM((tt, e), F32)],
        compiler_params=_params(("parallel", "arbitrary")),
        name="conf_layer",
    )(x, g_pre.reshape(1, d), w_in, conv_w, row(conv_b), row(ln_g), row(ln_b), buf)


def _rope_tables(pos0, t):
    inv = 1.0 / (ROPE_BASE ** (jnp.arange(0, C_ROPE, 2, dtype=F32) / C_ROPE))
    ang = (pos0 + jnp.arange(t, dtype=jnp.int32)).astype(F32)[:, None] * inv[None, :]
    cos, sin = jnp.cos(ang), jnp.sin(ang)
    return jnp.tile(cos, (1, 4)), jnp.tile(jnp.concatenate([-sin, sin], axis=-1), (1, 2))


def _pad_keys(k, kb):
    tk = k.shape[1]
    tkp = -(-tk // kb) * kb
    return jnp.pad(k.astype(BF16), ((0, 0), (0, tkp - tk), (0, 0)))


def _trunk(x, w, st, *, pos0, reset_first, chunked):
    b, t, d = x.shape
    m = b * t
    new = {}

    def layer(x, idx, mix):
        h = mix(x, w["norm_pre"][idx], w["w_in"][idx])
        return _out_res(h.reshape(m, E_WIDTH), w["w_out"][idx], w["norm_post"][idx], x.reshape(m, d)).reshape(b, t, d)

    def mix_a(x, g_pre, w_in):
        h, new["hgrn"] = _hgrn_layer(x, g_pre, w_in, w["a_lb_logits"], 0, w["a_norm_g"], st["hgrn"])
        return h

    def mix_b(x, g_pre, w_in):
        h, hl, new["rg_conv"] = _rglru_layer(x, g_pre, w_in, w["b_conv_w"], w["b_conv_b"], w["b_wa"], w["b_ba"],
                                             w["b_wx"], w["b_bx"], w["b_lambda"], st["rg_h"], st["rg_conv"],
                                             reset_first)
        new["rg_h"] = hl.reshape(b, E_WIDTH)
        return h

    def mix_c(x, g_pre, w_in):
        cos, sin = _rope_tables(pos0, t)
        gate, qa, qp, ckv, kpe = _mla_q(x, g_pre, w_in, cos, sin, w["c_q_norm"], w["c_kv_norm"], w["c_w_nope"],
                                        w["c_w_pe"], w["c_w_pe_sw"], w["c_w_uk_h"])
        new["mla_c"], new["mla_pe"] = ckv, kpe
        kc, kp = ckv, kpe
        n_prefix = 0
        if st["mla_c"] is not None:
            pc, pp = st["mla_c"], st["mla_pe"]
            n_prefix = pc.shape[1]
            pc = jnp.broadcast_to(pc, (b,) + pc.shape[1:])
            pp = jnp.broadcast_to(pp, (b,) + pp.shape[1:])
            kc = jnp.concatenate([pc.astype(BF16), ckv.astype(BF16)], axis=1)
            kp = jnp.concatenate([pp.astype(BF16), kpe.astype(BF16)], axis=1)
        return _mla_attn(qa, qp, kc, kp, gate, w["c_w_uv_h"], n_prefix, chunked)

    def mix_d(x, g_pre, w_in):
        h, new["conf"] = _conf_layer(x, g_pre, w_in, w["d_conv_w"], w["d_conv_b"], w["d_ln_g"], w["d_ln_b"],
                                     st["conf"])
        return h

    for idx, mix in enumerate((mix_a, mix_b, mix_c, mix_d)):
        x = layer(x, idx, mix)
    return x, new


def kernel(x_prompt, x_sample, state_hgrn, state_rglru_h, state_rglru_conv, cache_mla_latent, cache_mla_rope, state_conformer_conv, meta_tokens, norm_pre, norm_post, a_w_in, a_lb_logits, a_norm_g, a_w_out, b_w_in, b_conv_w, b_conv_b, b_wa, b_ba, b_wx, b_bx, b_lambda, b_w_out, c_w_in, c_q_norm, c_kv_norm, c_w_uq, c_w_uk, c_w_uv, c_w_out, d_w_in, d_conv_w, d_conv_b, d_ln_g, d_ln_b, d_w_out):
    assert norm_pre.shape[0] == 4, "one layer of each mixer type"
    bf = lambda a: a.astype(BF16)

    c_in = c_w_in[0]
    i1, i2, i3 = C_Q_LORA, C_Q_LORA + C_KV_LORA, C_Q_LORA + C_KV_LORA + C_ROPE
    half = C_ROPE // 2
    k_pe_cols = c_in[:, i2:i3]
    k_pe_sw = jnp.concatenate([k_pe_cols[:, half:], k_pe_cols[:, :half]], axis=1)
    c_in_perm = jnp.concatenate([c_in[:, i3:], c_in[:, :i3], k_pe_sw], axis=1)
    uq = c_w_uq[0].reshape(C_Q_LORA, C_HEADS, C_NOPE + C_ROPE)
    uq_pe = uq[:, :, C_NOPE:]
    uq_pe_sw = jnp.concatenate([uq_pe[..., half:], uq_pe[..., :half]], axis=-1)

    w = {
        "norm_pre": norm_pre, "norm_post": norm_post,
        "w_in": [bf(a_w_in[0]), bf(b_w_in[0]), bf(c_in_perm), bf(d_w_in[0])],
        "w_out": [bf(a_w_out[0]), bf(b_w_out[0]), bf(c_w_out[0]), bf(d_w_out[0])],
        "a_lb_logits": a_lb_logits, "a_norm_g": a_norm_g[0],
        "b_conv_w": b_conv_w[0], "b_conv_b": b_conv_b[0], "b_wa": bf(b_wa[0]), "b_ba": b_ba[0],
        "b_wx": bf(b_wx[0]), "b_bx": b_bx[0], "b_lambda": b_lambda[0],
        "c_q_norm": c_q_norm[0], "c_kv_norm": c_kv_norm[0],
        "c_w_nope": bf(uq[:, :, :C_NOPE].reshape(C_Q_LORA, C_HEADS * C_NOPE)),
        "c_w_pe": bf(uq_pe.reshape(C_Q_LORA, C_HEADS * C_ROPE)),
        "c_w_pe_sw": bf(uq_pe_sw.reshape(C_Q_LORA, C_HEADS * C_ROPE)),
        "c_w_uk_h": bf(jnp.transpose(c_w_uk[0].reshape(C_KV_LORA, C_HEADS, C_NOPE), (1, 2, 0))),
        "c_w_uv_h": bf(jnp.transpose(c_w_uv[0].reshape(C_KV_LORA, C_HEADS, C_V), (1, 0, 2))),
        "d_conv_w": d_conv_w[0], "d_conv_b": d_conv_b[0], "d_ln_g": d_ln_g[0], "d_ln_b": d_ln_b[0],
    }

    bp = x_prompt.shape[0]
    dt = x_prompt.dtype

    st_m = {"hgrn": jnp.zeros((1, A_HEADS, A_D, A_D), dt), "rg_h": jnp.zeros((1, E_WIDTH), dt),
            "rg_conv": jnp.zeros((1, B_CONV - 1, E_WIDTH), dt), "mla_c": None, "mla_pe": None,
            "conf": jnp.zeros((1, D_CONV - 1, E_WIDTH), dt)}
    _, new_m = _trunk(meta_tokens.astype(dt)[None], w, st_m, pos0=0, reset_first=True, chunked=False)

    st_p = {"hgrn": new_m["hgrn"], "rg_h": new_m["rg_h"], "rg_conv": new_m["rg_conv"],
            "mla_c": new_m["mla_c"], "mla_pe": new_m["mla_pe"], "conf": new_m["conf"]}
    yp, new_p = _trunk(x_prompt, w, st_p, pos0=N_META, reset_first=False, chunked=True)

    st_s = {"hgrn": state_hgrn[0], "rg_h": state_rglru_h[0], "rg_conv": state_rglru_conv[0],
            "mla_c": cache_mla_latent[0], "mla_pe": cache_mla_rope[0], "conf": state_conformer_conv[0]}
    ys, new_s = _trunk(x_sample, w, st_s, pos0=cache_mla_latent.shape[2], reset_first=False, chunked=False)

    def with_meta(meta_rows, rows):
        return jnp.concatenate([jnp.broadcast_to(meta_rows, (bp,) + meta_rows.shape[1:]), rows], axis=1)

    return (yp, ys,
            new_p["hgrn"][None], new_s["hgrn"][None],
            new_p["rg_h"][None], new_s["rg_h"][None],
            new_p["rg_conv"][None], new_s["rg_conv"][None],
            with_meta(new_m["mla_c"], new_p["mla_c"])[None], new_s["mla_c"][None],
            with_meta(new_m["mla_pe"], new_p["mla_pe"])[None], new_s["mla_pe"][None],
            new_p["conf"][None], new_s["conf"][None])
```
